```python
import jax, jax.numpy as jnp
from jax import lax
import numpy as np

D_MODEL = 1024
BATCH = 8
SEQ = 4096
DEPTH = 1

HEAD_DIM = 64
SB_HEADS = 8
SW_HEADS = 8
SW_KV_HEADS = 2
SB_WIDTH = SB_HEADS * HEAD_DIM
SW_WIDTH = SW_HEADS * HEAD_DIM
SW_KV_WIDTH = SW_KV_HEADS * HEAD_DIM
MIX_WIDTH = SB_WIDTH + SW_WIDTH
WINDOW = 128
BLOCK = 128
RMS_EPS = 1e-6
NEG_INF = -1e30
IN_SPLIT_SIZES = (SB_WIDTH, SB_WIDTH, SB_WIDTH, SB_WIDTH, SW_WIDTH, SW_KV_WIDTH, SW_KV_WIDTH, SW_WIDTH)
IN_WIDTH = 4 * SB_WIDTH + 2 * SW_WIDTH + 2 * SW_KV_WIDTH

kernel_name = "hymba_stickbreak_swa_sink_adaln"


def rmsnorm(x, g):
    xf = x.astype(jnp.float32)
    y = xf * lax.rsqrt(jnp.mean(xf * xf, axis=-1, keepdims=True) + RMS_EPS)
    return (y * g.astype(jnp.float32)).astype(x.dtype)


def alibi_slopes(n):
    return jnp.asarray([2.0 ** (-8.0 * (h + 1) / n) for h in range(n)], dtype=jnp.float32)


def stick_breaking_attention(q, k, v):
    S, Dh = q.shape[1], q.shape[3]
    scale = Dh ** -0.5
    outs = []
    for start in range(0, S, BLOCK):
        end = start + BLOCK
        qb = q[:, start:end]
        kb = k[:, :end]
        vb = v[:, :end]
        z = jnp.einsum('bqhd,bkhd->bhqk', qb, kb).astype(jnp.float32) * scale
        qpos = start + jnp.arange(BLOCK)[:, None]
        kpos = jnp.arange(end)[None, :]
        causal = kpos < qpos
        log_beta = jax.nn.log_sigmoid(z)
        log_1mb = jnp.where(causal, jax.nn.log_sigmoid(-z), 0.0)
        later = lax.cumsum(log_1mb, axis=3, reverse=True) - log_1mb
        w = jnp.where(causal, jnp.exp(log_beta + later), 0.0)
        outs.append(jnp.einsum('bhqk,bkhd->bqhd', w.astype(v.dtype), vb))
    return jnp.concatenate(outs, axis=1)


def sliding_window_sink_attention(q, k, v, sinks, slopes):
    B, S, H, Dh = q.shape
    Hkv = k.shape[2]
    G = H // Hkv
    nb = S // BLOCK
    qb = q.reshape(B, nb, BLOCK, Hkv, G, Dh)

    def band(t):
        tb = t.reshape(B, nb, BLOCK, Hkv, Dh)
        prev = jnp.pad(tb[:, :-1], ((0, 0), (1, 0), (0, 0), (0, 0), (0, 0)))
        return jnp.concatenate([prev, tb], axis=2)

    kb, vb = band(k), band(v)
    s = jnp.einsum('bnqhgd,bnshd->bhgnqs', qb, kb).astype(jnp.float32) * (Dh ** -0.5)
    r = jnp.arange(BLOCK)[:, None]
    j = jnp.arange(2 * BLOCK)[None, :]
    rel = (BLOCK + r - j)
    kpos = (jnp.arange(nb)[:, None, None] - 1) * BLOCK + j[None]
    valid = (rel >= 0)[None] & (rel < WINDOW)[None] & (kpos >= 0)
    m = slopes.reshape(Hkv, G)[:, :, None, None, None]
    logits = s - m * rel.astype(jnp.float32)
    logits = jnp.where(valid, logits, NEG_INF)
    sink = jnp.broadcast_to(sinks.astype(jnp.float32).reshape(1, Hkv, G, 1, 1, 1),
                            logits.shape[:-1] + (1,))
    p = jax.nn.softmax(jnp.concatenate([logits, sink], axis=-1), axis=-1)[..., :-1]
    o = jnp.einsum('bhgnqs,bnshd->bnqhgd', p.astype(v.dtype), vb)
    return o.reshape(B, S, H, Dh)


def setup_inputs(seed: int = 0) -> dict:
    key = jax.random.key(seed)
    ks = jax.random.split(key, 11)
    D = D_MODEL
    x = jax.random.normal(ks[0], (BATCH, SEQ, D), jnp.float32)
    c = jax.random.normal(ks[1], (BATCH, D), jnp.float32)
    w_ada = jax.random.normal(ks[2], (DEPTH, D, 3 * D), jnp.float32) * (0.5 * D ** -0.5)
    b_ada = jax.random.normal(ks[3], (DEPTH, 3 * D), jnp.float32) * 0.01
    norm_g = 1.0 + 0.01 * jax.random.normal(ks[4], (DEPTH, D), jnp.float32)
    w_in = jax.random.normal(ks[5], (DEPTH, D, IN_WIDTH), jnp.float32) * (D ** -0.5)
    sinks = jax.random.normal(ks[6], (DEPTH, SW_HEADS), jnp.float32) * 0.5
    w_out = jax.random.normal(ks[7], (DEPTH, MIX_WIDTH, D), jnp.float32) * (MIX_WIDTH ** -0.5)
    final_g = 1.0 + 0.01 * jax.random.normal(ks[8], (D,), jnp.float32)
    return {"x": x, "c": c, "w_ada": w_ada, "b_ada": b_ada, "norm_g": norm_g,
            "w_in": w_in, "sinks": sinks, "w_out": w_out, "final_g": final_g}


def reference(x, c, w_ada, b_ada, norm_g, w_in, sinks, w_out, final_g):
    B, S, _ = x.shape
    slopes = alibi_slopes(SW_HEADS)
    offsets = np.cumsum(IN_SPLIT_SIZES)[:-1].tolist()
    cond = jax.nn.silu(c)
    for l in range(DEPTH):
        mod = cond @ w_ada[l] + b_ada[l]
        shift, scale, gate = jnp.split(mod, 3, axis=-1)
        h = rmsnorm(x, norm_g[l]) * (1.0 + scale[:, None, :]) + shift[:, None, :]
        proj = h @ w_in[l]
        sb_q, sb_k, sb_v, sb_g, sw_q, sw_k, sw_v, sw_g = jnp.split(proj, offsets, axis=-1)
        y_sb = stick_breaking_attention(
            sb_q.reshape(B, S, SB_HEADS, HEAD_DIM),
            sb_k.reshape(B, S, SB_HEADS, HEAD_DIM),
            sb_v.reshape(B, S, SB_HEADS, HEAD_DIM)).reshape(B, S, SB_WIDTH)
        y_sb = y_sb * jax.nn.silu(sb_g)
        y_sw = sliding_window_sink_attention(
            sw_q.reshape(B, S, SW_HEADS, HEAD_DIM),
            sw_k.reshape(B, S, SW_KV_HEADS, HEAD_DIM),
            sw_v.reshape(B, S, SW_KV_HEADS, HEAD_DIM),
            sinks[l], slopes).reshape(B, S, SW_WIDTH)
        y_sw = y_sw * jax.nn.silu(sw_g)
        y = jnp.concatenate([y_sb, y_sw], axis=-1) @ w_out[l]
        x = x + gate[:, None, :] * y
    return rmsnorm(x, final_g)
```

```python
import jax
import jax.numpy as jnp
from jax import lax
from jax.experimental import pallas as pl
from jax.experimental.pallas import tpu as pltpu

D_MODEL = 1024
HEAD_DIM = 64
SB_HEADS = 8
SW_HEADS = 8
SW_KV_HEADS = 2
SB_WIDTH = SB_HEADS * HEAD_DIM
SW_WIDTH = SW_HEADS * HEAD_DIM
SW_KV_WIDTH = SW_KV_HEADS * HEAD_DIM
IN_WIDTH = 4 * SB_WIDTH + 2 * SW_WIDTH + 2 * SW_KV_WIDTH
WINDOW = 128
RMS_EPS = 1e-6
NEG_INF = -1e30

LANES = 128
PAIRS = SB_WIDTH // LANES
SB_Q_BLK, SB_K_BLK, SB_V_BLK, SB_G_BLK = 0, PAIRS, 2 * PAIRS, 3 * PAIRS
SW_Q_BLK = 4 * PAIRS
SW_K_BLK = SW_Q_BLK + PAIRS
SW_V_BLK = SW_K_BLK + 1
SW_G_BLK = SW_V_BLK + 1

ROW_BLOCK = 512
PROJ_COLS = 256
SB_TILE = 256
VMEM_LIMIT = 48 * 1024 * 1024

_F32 = jnp.float32
_BF16 = jnp.bfloat16


def _silu(v):
    return v / (1.0 + jnp.exp(-v))


def _adaln_kernel(c_ref, w_ref, b_ref, o_ref):
    cond = _silu(c_ref[...])
    o_ref[...] = jnp.dot(cond, w_ref[...], preferred_element_type=_F32) + b_ref[...]


def _adaln(c, w_ada, b_ada):
    bsz, d = c.shape
    n = w_ada.shape[1]
    bn = 512
    return pl.pallas_call(
        _adaln_kernel,
        grid=(n // bn,),
        in_specs=[
            pl.BlockSpec((bsz, d), lambda j: (0, 0)),
            pl.BlockSpec((d, bn), lambda j: (0, j)),
            pl.BlockSpec((1, bn), lambda j: (0, j)),
        ],
        out_specs=pl.BlockSpec((bsz, bn), lambda j: (0, j)),
        out_shape=jax.ShapeDtypeStruct((bsz, n), _F32),
        name="adaln_mod",
    )(c, w_ada, b_ada.reshape(1, n))


def _inproj_kernel(x_ref, mod_ref, g_ref, w_ref, o_ref):
    x = x_ref[...]
    r = lax.rsqrt(jnp.mean(x * x, axis=-1, keepdims=True) + RMS_EPS)
    shift = mod_ref[:, 0:D_MODEL]
    scale = mod_ref[:, D_MODEL:2 * D_MODEL]
    h = ((x * r) * g_ref[...]) * (1.0 + scale) + shift
    hb = h.astype(_BF16)
    for n0 in range(0, IN_WIDTH, PROJ_COLS):
        o_ref[:, n0:n0 + PROJ_COLS] = jnp.dot(
            hb, w_ref[:, n0:n0 + PROJ_COLS], preferred_element_type=_F32).astype(_BF16)


def _inproj(x, mod3, norm_g, w_in_bf16):
    bsz, s, d = x.shape
    return pl.pallas_call(
        _inproj_kernel,
        grid=(bsz, s // ROW_BLOCK),
        in_specs=[
            pl.BlockSpec((None, ROW_BLOCK, d), lambda b, i: (b, i, 0)),
            pl.BlockSpec((None, 1, 3 * d), lambda b, i: (b, 0, 0)),
            pl.BlockSpec((1, d), lambda b, i: (0, 0)),
            pl.BlockSpec((d, IN_WIDTH), lambda b, i: (0, 0)),
        ],
        out_specs=pl.BlockSpec((None, ROW_BLOCK, IN_WIDTH), lambda b, i: (b, i, 0)),
        out_shape=jax.ShapeDtypeStruct((bsz, s, IN_WIDTH), _BF16),
        compiler_params=pltpu.CompilerParams(
            dimension_semantics=("parallel", "parallel"), vmem_limit_bytes=VMEM_LIMIT),
        name="in_proj",
    )(x, mod3, norm_g.reshape(1, d), w_in_bf16)


def _sb_kernel(q_ref, k_ref, v_ref, g_ref, o_ref, u_ref, acc_ref, carry_ref):
    t = SB_TILE
    seq = q_ref.shape[0]
    row = lax.broadcasted_iota(jnp.int32, (t, t), 0)
    col = lax.broadcasted_iota(jnp.int32, (t, t), 1)
    u_ref[...] = jnp.where(row >= col, 1.0, 0.0).astype(_BF16)
    row2 = lax.broadcasted_iota(jnp.int32, (2 * t, t), 0) & (t - 1)
    col2 = lax.broadcasted_iota(jnp.int32, (2 * t, t), 1)
    causal = col2 < row2
    head_a = lax.broadcasted_iota(jnp.int32, (t, LANES), 1) < HEAD_DIM

    def tile(q2, kj, diag):
        ks = pl.multiple_of(kj * t, t)
        k = k_ref[pl.ds(ks, t), :]
        v = v_ref[pl.ds(ks, t), :]
        z = lax.dot_general(q2, k, (((1,), (1,)), ((), ())), preferred_element_type=_F32)
        p = jnp.maximum(z, 0.0) + jnp.log(1.0 + jnp.exp(-jnp.abs(z)))
        if diag:
            p = jnp.where(causal, p, 0.0)
        suffix = jnp.dot(p.astype(_BF16), u_ref[...], preferred_element_type=_F32)
        w = jnp.exp(z - suffix)
        if diag:
            w = jnp.where(causal, w, 0.0)
        o = jnp.dot(w.astype(_BF16), v, preferred_element_type=_F32)
        carry = carry_ref[...]
        acc_ref[...] += jnp.exp(carry) * o
        carry_ref[...] = carry - jnp.sum(p, axis=1, keepdims=True)

    def q_block(qi, _):
        qs = pl.multiple_of(qi * t, t)
        q = q_ref[pl.ds(qs, t), :] * jnp.asarray(HEAD_DIM ** -0.5, _BF16)
        zero = jnp.zeros_like(q)
        q2 = jnp.concatenate([jnp.where(head_a, q, zero), jnp.where(head_a, zero, q)], axis=0)
        acc_ref[...] = jnp.zeros_like(acc_ref)
        carry_ref[...] = jnp.zeros_like(carry_ref)
        tile(q2, qi, True)

        def k_block(kk, _):
            tile(q2, qi - 1 - kk, False)
            return 0

        lax.fori_loop(0, qi, k_block, 0)
        acc = acc_ref[...]
        y = jnp.where(head_a, acc[0:t], acc[t:2 * t])
        g = g_ref[pl.ds(qs, t), :].astype(_F32)
        o_ref[pl.ds(qs, t), :] = (y * _silu(g)).astype(_BF16)
        return 0

    lax.fori_loop(0, seq // t, q_block, 0)


def _sb_attention(proj):
    bsz, s, _ = proj.shape
    t = SB_TILE

    def spec(blk):
        return pl.BlockSpec((None, s, LANES), lambda b, j, blk=blk: (b, 0, blk + j))

    return pl.pallas_call(
        _sb_kernel,
        grid=(bsz, PAIRS),
        in_specs=[spec(SB_Q_BLK), spec(SB_K_BLK), spec(SB_V_BLK), spec(SB_G_BLK)],
        out_specs=pl.BlockSpec((None, s, LANES), lambda b, j: (b, 0, j)),
        out_shape=jax.ShapeDtypeStruct((bsz, s, SB_WIDTH), _BF16),
        scratch_shapes=[
            pltpu.VMEM((t, t), _BF16),
            pltpu.VMEM((2 * t, LANES), _F32),
            pltpu.VMEM((2 * t, 1), _F32),
        ],
        compiler_params=pltpu.CompilerParams(
            dimension_semantics=("parallel", "parallel"), vmem_limit_bytes=VMEM_LIMIT),
        name="sb_attention",
    )(proj, proj, proj, proj)


def _sw_kernel(sinks_ref, slopes_ref, q_ref, k_ref, v_ref, g_ref, o_ref, bias_ref):
    w = WINDOW
    seq = q_ref.shape[0]
    j = pl.program_id(1)
    kv_first = j < PAIRS // 2
    lane = lax.broadcasted_iota(jnp.int32, (w, LANES), 1)
    head_a = lane < HEAD_DIM
    kv_lanes = head_a == kv_first

    row = lax.broadcasted_iota(jnp.int32, (2 * w, 2 * w), 0)
    col = lax.broadcasted_iota(jnp.int32, (2 * w, 2 * w), 1)
    rel = w + (row & (w - 1)) - col
    slope = jnp.where(row < w, slopes_ref[2 * j], slopes_ref[2 * j + 1])
    valid = (rel >= 0) & (rel < w)
    bias_ref[...] = jnp.where(valid, -slope * rel.astype(_F32), NEG_INF)
    rowc = lax.broadcasted_iota(jnp.int32, (2 * w, 1), 0)
    sink = jnp.where(rowc < w, sinks_ref[2 * j], sinks_ref[2 * j + 1])

    def q_block(n, first):
        qs = pl.multiple_of(n * w, w)
        q = q_ref[pl.ds(qs, w), :] * jnp.asarray(HEAD_DIM ** -0.5, _BF16)
        q_sw = pltpu.roll(q, HEAD_DIM, 1)
        zero = jnp.zeros_like(q)
        q_a = jnp.where(kv_lanes, jnp.where(kv_first, q, q_sw), zero)
        q_b = jnp.where(kv_lanes, jnp.where(kv_first, q_sw, q), zero)
        q2 = jnp.concatenate([q_a, q_b], axis=0)
        if first:
            k = k_ref[0:w, :]
            v = v_ref[0:w, :]
            bias = bias_ref[:, w:2 * w]
        else:
            ks = pl.multiple_of(qs - w, w)
            k = k_ref[pl.ds(ks, 2 * w), :]
            v = v_ref[pl.ds(ks, 2 * w), :]
            bias = bias_ref[...]
        logits = lax.dot_general(q2, k, (((1,), (1,)), ((), ())), preferred_element_type=_F32) + bias
        mx = jnp.maximum(jnp.max(logits, axis=1, keepdims=True), sink)
        p = jnp.exp(logits - mx)
        den = jnp.sum(p, axis=1, keepdims=True) + jnp.exp(sink - mx)
        o = jnp.dot(p.astype(_BF16), v, preferred_element_type=_F32) / den
        o_a, o_b = o[0:w], o[w:2 * w]
        o_a_sw = pltpu.roll(o_a, HEAD_DIM, 1)
        o_b_sw = pltpu.roll(o_b, HEAD_DIM, 1)
        y = jnp.where(head_a, jnp.where(kv_first, o_a, o_a_sw), jnp.where(kv_first, o_b_sw, o_b))
        g = g_ref[pl.ds(qs, w), :].astype(_F32)
        o_ref[pl.ds(qs, w), :] = (y * _silu(g)).astype(_BF16)

    q_block(0, True)

    def body(n, _):
        q_block(n, False)
        return 0

    lax.fori_loop(1, seq // w, body, 0)


def _sw_attention(proj, sinks, slopes):
    bsz, s, _ = proj.shape

    def spec(blk, per_pair):
        if per_pair:
            return pl.BlockSpec((None, s, LANES), lambda b, j, blk=blk: (b, 0, blk + j))
        return pl.BlockSpec((None, s, LANES), lambda b, j, blk=blk: (b, 0, blk))

    smem = pl.BlockSpec(memory_space=pltpu.SMEM)
    return pl.pallas_call(
        _sw_kernel,
        grid=(bsz, PAIRS),
        in_specs=[smem, smem, spec(SW_Q_BLK, True), spec(SW_K_BLK, False), spec(SW_V_BLK, False),
                  spec(SW_G_BLK, True)],
        out_specs=pl.BlockSpec((None, s, LANES), lambda b, j: (b, 0, j)),
        out_shape=jax.ShapeDtypeStruct((bsz, s, SW_WIDTH), _BF16),
        scratch_shapes=[pltpu.VMEM((2 * WINDOW, 2 * WINDOW), _F32)],
        compiler_params=pltpu.CompilerParams(
            dimension_semantics=("parallel", "parallel"), vmem_limit_bytes=VMEM_LIMIT),
        name="sw_attention",
    )(sinks, slopes, proj, proj, proj, proj)


def _outproj_kernel(x_ref, ysb_ref, ysw_ref, mod_ref, w_ref, fg_ref, o_ref):
    y = jnp.dot(ysb_ref[...], w_ref[0:SB_WIDTH, :], preferred_element_type=_F32)
    y = y + jnp.dot(ysw_ref[...], w_ref[SB_WIDTH:SB_WIDTH + SW_WIDTH, :], preferred_element_type=_F32)
    gate = mod_ref[:, 2 * D_MODEL:3 * D_MODEL]
    xn = x_ref[...] + gate * y
    r = lax.rsqrt(jnp.mean(xn * xn, axis=-1, keepdims=True) + RMS_EPS)
    o_ref[...] = (xn * r) * fg_ref[...]


def _outproj(x, y_sb, y_sw, mod3, w_out_bf16, final_g):
    bsz, s, d = x.shape
    return pl.pallas_call(
        _outproj_kernel,
        grid=(bsz, s // ROW_BLOCK),
        in_specs=[
            pl.BlockSpec((None, ROW_BLOCK, d), lambda b, i: (b, i, 0)),
            pl.BlockSpec((None, ROW_BLOCK, SB_WIDTH), lambda b, i: (b, i, 0)),
            pl.BlockSpec((None, ROW_BLOCK, SW_WIDTH), lambda b, i: (b, i, 0)),
            pl.BlockSpec((None, 1, 3 * d), lambda b, i: (b, 0, 0)),
            pl.BlockSpec((SB_WIDTH + SW_WIDTH, d), lambda b, i: (0, 0)),
            pl.BlockSpec((1, d), lambda b, i: (0, 0)),
        ],
        out_specs=pl.BlockSpec((None, ROW_BLOCK, d), lambda b, i: (b, i, 0)),
        out_shape=jax.ShapeDtypeStruct((bsz, s, d), _F32),
        compiler_params=pltpu.CompilerParams(
            dimension_semantics=("parallel", "parallel"), vmem_limit_bytes=VMEM_LIMIT),
        name="out_proj",
    )(x, y_sb, y_sw, mod3, w_out_bf16, final_g.reshape(1, d))


def kernel(x, c, w_ada, b_ada, norm_g, w_in, sinks, w_out, final_g):
    assert w_ada.shape[0] == 1, "the output projection kernel fuses the final norm: single layer only"
    bsz = x.shape[0]
    slopes = jnp.asarray([2.0 ** (-8.0 * (h + 1) / SW_HEADS) for h in range(SW_HEADS)], dtype=_F32)
    mod3 = _adaln(c, w_ada[0], b_ada[0]).reshape(bsz, 1, 3 * D_MODEL)
    proj = _inproj(x, mod3, norm_g[0], w_in[0].astype(_BF16))
    y_sb = _sb_attention(proj)
    y_sw = _sw_attention(proj, sinks[0], slopes)
    return _outproj(x, y_sb, y_sw, mod3, w_out[0].astype(_BF16), final_g)
```

```python
import jax
import jax.numpy as jnp
from jax import lax
from jax.experimental import pallas as pl
from jax.experimental.pallas import tpu as pltpu

D_MODEL = 1024
HEAD_DIM = 64
SB_HEADS = 8
SW_HEADS = 8
SW_KV_HEADS = 2
SB_WIDTH = SB_HEADS * HEAD_DIM
SW_WIDTH = SW_HEADS * HEAD_DIM
SW_KV_WIDTH = SW_KV_HEADS * HEAD_DIM
IN_WIDTH = 4 * SB_WIDTH + 2 * SW_WIDTH + 2 * SW_KV_WIDTH
WINDOW = 128
RMS_EPS = 1e-6
NEG_INF = -1e30

LANES = 128
PAIRS = SB_WIDTH // LANES
SB_Q_BLK, SB_K_BLK, SB_V_BLK, SB_G_BLK = 0, PAIRS, 2 * PAIRS, 3 * PAIRS
SW_Q_BLK = 4 * PAIRS
SW_K_BLK = SW_Q_BLK + PAIRS
SW_V_BLK = SW_K_BLK + 1
SW_G_BLK = SW_V_BLK + 1

ROW_BLOCK = 512
OUT_ROW_BLOCK = 1024
PROJ_COLS = 256
SB_TILE = 256
SW_PREP_ROWS = 512
SW_GROUP = 8
SB_SKIP = 105.0
VMEM_LIMIT = 48 * 1024 * 1024

_F32 = jnp.float32
_BF16 = jnp.bfloat16


def _silu(v):
    return v / (1.0 + jnp.exp(-v))


def _adaln_kernel(c_ref, w_ref, b_ref, o_ref):
    cond = _silu(c_ref[...])
    o_ref[...] = jnp.dot(cond, w_ref[...], preferred_element_type=_F32) + b_ref[...]


def _adaln(c, w_ada, b_ada):
    bsz, d = c.shape
    n = w_ada.shape[1]
    bn = 512
    return pl.pallas_call(
        _adaln_kernel,
        grid=(n // bn,),
        in_specs=[
            pl.BlockSpec((bsz, d), lambda j: (0, 0)),
            pl.BlockSpec((d, bn), lambda j: (0, j)),
            pl.BlockSpec((1, bn), lambda j: (0, j)),
        ],
        out_specs=pl.BlockSpec((bsz, bn), lambda j: (0, j)),
        out_shape=jax.ShapeDtypeStruct((bsz, n), _F32),
        name="adaln_mod",
    )(c, w_ada, b_ada.reshape(1, n))


def _inproj_kernel(x_ref, mod_ref, g_ref, w_ref, o_ref):
    x = x_ref[...]
    r = lax.rsqrt(jnp.mean(x * x, axis=-1, keepdims=True) + RMS_EPS)
    shift = mod_ref[:, 0:D_MODEL]
    scale = mod_ref[:, D_MODEL:2 * D_MODEL]
    h = ((x * r) * g_ref[...]) * (1.0 + scale) + shift
    hb = h.astype(_BF16)
    for n0 in range(0, IN_WIDTH, PROJ_COLS):
        o_ref[:, n0:n0 + PROJ_COLS] = jnp.dot(
            hb, w_ref[:, n0:n0 + PROJ_COLS], preferred_element_type=_F32).astype(_BF16)


def _inproj(x, mod3, norm_g, w_in_bf16):
    bsz, s, d = x.shape
    return pl.pallas_call(
        _inproj_kernel,
        grid=(bsz, s // ROW_BLOCK),
        in_specs=[
            pl.BlockSpec((None, ROW_BLOCK, d), lambda b, i: (b, i, 0)),
            pl.BlockSpec((None, 1, 3 * d), lambda b, i: (b, 0, 0)),
            pl.BlockSpec((1, d), lambda b, i: (0, 0)),
            pl.BlockSpec((d, IN_WIDTH), lambda b, i: (0, 0)),
        ],
        out_specs=pl.BlockSpec((None, ROW_BLOCK, IN_WIDTH), lambda b, i: (b, i, 0)),
        out_shape=jax.ShapeDtypeStruct((bsz, s, IN_WIDTH), _BF16),
        compiler_params=pltpu.CompilerParams(
            dimension_semantics=("parallel", "parallel"), vmem_limit_bytes=VMEM_LIMIT),
        name="in_proj",
    )(x, mod3, norm_g.reshape(1, d), w_in_bf16)


def _sb_kernel(q_ref, k_ref, v_ref, g_ref, o_ref, u_ref, q2_ref, acc_ref, carry_ref):
    t = SB_TILE
    seq = q_ref.shape[0]
    row = lax.broadcasted_iota(jnp.int32, (t, t), 0)
    col = lax.broadcasted_iota(jnp.int32, (t, t), 1)
    u_ref[...] = jnp.where(row >= col, 1.0, 0.0).astype(_BF16)
    row2 = lax.broadcasted_iota(jnp.int32, (2 * t, t), 0) & (t - 1)
    col2 = lax.broadcasted_iota(jnp.int32, (2 * t, t), 1)
    causal = col2 < row2
    head_a = lax.broadcasted_iota(jnp.int32, (t, LANES), 1) < HEAD_DIM

    def tile(kj, diag):
        ks = pl.multiple_of(kj * t, t)
        k = k_ref[pl.ds(ks, t), :]
        v = v_ref[pl.ds(ks, t), :]
        z = lax.dot_general(q2_ref[...], k, (((1,), (1,)), ((), ())), preferred_element_type=_F32)
        p = jnp.maximum(z, 0.0) + jnp.log(1.0 + jnp.exp(-jnp.abs(z)))
        if diag:
            p = jnp.where(causal, p, 0.0)
        suffix = jnp.dot(p.astype(_BF16), u_ref[...], preferred_element_type=_F32)
        w = jnp.exp(z - suffix)
        if diag:
            w = jnp.where(causal, w, 0.0)
        o = jnp.dot(w.astype(_BF16), v, preferred_element_type=_F32)
        return o, jnp.sum(p, axis=1, keepdims=True)

    def q_block(qi, has_prev):
        qs = pl.multiple_of(qi * t, t)
        q = q_ref[pl.ds(qs, t), :] * jnp.asarray(HEAD_DIM ** -0.5, _BF16)
        zero = jnp.zeros_like(q)
        q2_ref[...] = jnp.concatenate([jnp.where(head_a, q, zero), jnp.where(head_a, zero, q)], axis=0)
        o0, rs0 = tile(qi, True)
        if has_prev:
            o1, rs1 = tile(qi - 1, False)
            c1 = -rs0
            acc_ref[...] = o0 + jnp.exp(c1) * o1
            carry_ref[...] = c1 - rs1

            def more(state):
                kj, cmax = state
                return jnp.logical_and(kj >= 0, cmax > -SB_SKIP)

            def k_block(state):
                kj, _ = state
                o, rs = tile(kj, False)
                c = carry_ref[...]
                acc_ref[...] += jnp.exp(c) * o
                c = c - rs
                carry_ref[...] = c
                return kj - 1, jnp.max(c)

            lax.while_loop(more, k_block, (qi - 2, jnp.max(carry_ref[...])))
            acc = acc_ref[...]
        else:
            acc = o0
        y = jnp.where(head_a, acc[0:t], acc[t:2 * t])
        g = g_ref[pl.ds(qs, t), :].astype(_F32)
        o_ref[pl.ds(qs, t), :] = (y * _silu(g)).astype(_BF16)

    q_block(0, False)

    def body(qi, _):
        q_block(qi, True)
        return 0

    lax.fori_loop(1, seq // t, body, 0)


def _sb_attention(proj):
    bsz, s, _ = proj.shape
    t = SB_TILE

    def spec(blk):
        return pl.BlockSpec((None, s, LANES), lambda b, j, blk=blk: (b, 0, blk + j))

    return pl.pallas_call(
        _sb_kernel,
        grid=(bsz, PAIRS),
        in_specs=[spec(SB_Q_BLK), spec(SB_K_BLK), spec(SB_V_BLK), spec(SB_G_BLK)],
        out_specs=pl.BlockSpec((None, s, LANES), lambda b, j: (b, 0, j)),
        out_shape=jax.ShapeDtypeStruct((bsz, s, SB_WIDTH), _BF16),
        scratch_shapes=[
            pltpu.VMEM((t, t), _BF16),
            pltpu.VMEM((2 * t, LANES), _BF16),
            pltpu.VMEM((2 * t, LANES), _F32),
            pltpu.VMEM((2 * t, 1), _F32),
        ],
        compiler_params=pltpu.CompilerParams(
            dimension_semantics=("parallel", "parallel"), vmem_limit_bytes=VMEM_LIMIT),
        name="sb_attention",
    )(proj, proj, proj, proj)


def _sw_kernel(sinks_ref, slopes_ref, q_ref, k_ref, v_ref, g_ref, o_ref, bias_ref, ksel_ref, vext_ref):
    w = WINDOW
    seq = q_ref.shape[0]
    j = pl.program_id(1)
    kvh = j // (PAIRS // SW_KV_HEADS)
    lane = lax.broadcasted_iota(jnp.int32, (w, LANES), 1)
    head_a = lane < HEAD_DIM

    chunk = SW_PREP_ROWS
    low = lax.broadcasted_iota(jnp.int32, (chunk, LANES), 1) < HEAD_DIM

    @pl.when(j == 0)
    def _():
        def prep(i, _):
            rs = pl.multiple_of(i * chunk, chunk)
            kc = k_ref[pl.ds(rs, chunk), :]
            vc = v_ref[pl.ds(rs, chunk), :]
            kr = pltpu.roll(kc, HEAD_DIM, 1)
            vr = pltpu.roll(vc, HEAD_DIM, 1)
            ones = jnp.ones((chunk, LANES), _BF16)
            ksel_ref[0, pl.ds(rs, chunk), :] = jnp.where(low, kc, kr)
            ksel_ref[1, pl.ds(rs, chunk), :] = jnp.where(low, kr, kc)
            vext_ref[0, pl.ds(rs, chunk), 0:LANES] = jnp.where(low, vc, vr)
            vext_ref[1, pl.ds(rs, chunk), 0:LANES] = jnp.where(low, vr, vc)
            vext_ref[0, pl.ds(rs, chunk), LANES:2 * LANES] = ones
            vext_ref[1, pl.ds(rs, chunk), LANES:2 * LANES] = ones
            return 0

        lax.fori_loop(0, seq // chunk, prep, 0)

    row = lax.broadcasted_iota(jnp.int32, (2 * w, 2 * w), 0)
    col = lax.broadcasted_iota(jnp.int32, (2 * w, 2 * w), 1)
    rel = w + (row & (w - 1)) - col
    slope = jnp.where(row < w, slopes_ref[2 * j], slopes_ref[2 * j + 1])
    valid = (rel >= 0) & (rel < w)
    bias_ref[...] = jnp.where(valid, -slope * rel.astype(_F32), NEG_INF)
    rowc = lax.broadcasted_iota(jnp.int32, (2 * w, 1), 0)
    sink = jnp.where(rowc < w, sinks_ref[2 * j], sinks_ref[2 * j + 1])

    def q_block(n, first):
        qs = n * w if isinstance(n, int) else pl.multiple_of(n * w, w)
        q = q_ref[pl.ds(qs, w), :] * jnp.asarray(HEAD_DIM ** -0.5, _BF16)
        zero = jnp.zeros_like(q)
        q2 = jnp.concatenate([jnp.where(head_a, q, zero), jnp.where(head_a, zero, q)], axis=0)
        if first:
            k = ksel_ref[kvh, 0:w, :]
            v = vext_ref[kvh, 0:w, :]
            bias = bias_ref[:, w:2 * w]
        else:
            ks = qs - w if isinstance(n, int) else pl.multiple_of(qs - w, w)
            k = ksel_ref[kvh, pl.ds(ks, 2 * w), :]
            v = vext_ref[kvh, pl.ds(ks, 2 * w), :]
            bias = bias_ref[...]
        logits = lax.dot_general(q2, k, (((1,), (1,)), ((), ())), preferred_element_type=_F32) + bias
        mx = jnp.maximum(jnp.max(logits, axis=1, keepdims=True), sink)
        p = jnp.exp(logits - mx)
        oe = jnp.dot(p.astype(_BF16), v, preferred_element_type=_F32)
        y2 = oe[:, 0:LANES] / (oe[:, LANES:2 * LANES] + jnp.exp(sink - mx))
        y = jnp.where(head_a, y2[0:w], y2[w:2 * w])
        g = g_ref[pl.ds(qs, w), :].astype(_F32)
        o_ref[pl.ds(qs, w), :] = (y * _silu(g)).astype(_BF16)

    for n in range(SW_GROUP):
        q_block(n, n == 0)

    def body(i, _):
        for u in range(SW_GROUP):
            q_block(i * SW_GROUP + u, False)
        return 0

    lax.fori_loop(1, seq // (w * SW_GROUP), body, 0)


def _sw_attention(proj, sinks, slopes):
    bsz, s, _ = proj.shape

    def spec(blk, per_pair):
        if per_pair:
            return pl.BlockSpec((None, s, LANES), lambda b, j, blk=blk: (b, 0, blk + j))
        return pl.BlockSpec((None, s, LANES), lambda b, j, blk=blk: (b, 0, blk))

    smem = pl.BlockSpec(memory_space=pltpu.SMEM)
    return pl.pallas_call(
        _sw_kernel,
        grid=(bsz, PAIRS),
        in_specs=[smem, smem, spec(SW_Q_BLK, True), spec(SW_K_BLK, False), spec(SW_V_BLK, False),
                  spec(SW_G_BLK, True)],
        out_specs=pl.BlockSpec((None, s, LANES), lambda b, j: (b, 0, j)),
        out_shape=jax.ShapeDtypeStruct((bsz, s, SW_WIDTH), _BF16),
        scratch_shapes=[
            pltpu.VMEM((2 * WINDOW, 2 * WINDOW), _F32),
            pltpu.VMEM((SW_KV_HEADS, s, LANES), _BF16),
            pltpu.VMEM((SW_KV_HEADS, s, 2 * LANES), _BF16),
        ],
        compiler_params=pltpu.CompilerParams(
            dimension_semantics=("parallel", "arbitrary"), vmem_limit_bytes=VMEM_LIMIT),
        name="sw_attention",
    )(sinks, slopes, proj, proj, proj, proj)


def _outproj_kernel(x_ref, ysb_ref, ysw_ref, mod_ref, w_ref, fg_ref, o_ref):
    y = jnp.dot(ysb_ref[...], w_ref[0:SB_WIDTH, :], preferred_element_type=_F32)
    y = y + jnp.dot(ysw_ref[...], w_ref[SB_WIDTH:SB_WIDTH + SW_WIDTH, :], preferred_element_type=_F32)
    gate = mod_ref[:, 2 * D_MODEL:3 * D_MODEL]
    xn = x_ref[...] + gate * y
    r = lax.rsqrt(jnp.mean(xn * xn, axis=-1, keepdims=True) + RMS_EPS)
    o_ref[...] = (xn * r) * fg_ref[...]


def _outproj(x, y_sb, y_sw, mod3, w_out_bf16, final_g):
    bsz, s, d = x.shape
    return pl.pallas_call(
        _outproj_kernel,
        grid=(bsz, s // OUT_ROW_BLOCK),
        in_specs=[
            pl.BlockSpec((None, OUT_ROW_BLOCK, d), lambda b, i: (b, i, 0)),
            pl.BlockSpec((None, OUT_ROW_BLOCK, SB_WIDTH), lambda b, i: (b, i, 0)),
            pl.BlockSpec((None, OUT_ROW_BLOCK, SW_WIDTH), lambda b, i: (b, i, 0)),
            pl.BlockSpec((None, 1, 3 * d), lambda b, i: (b, 0, 0)),
            pl.BlockSpec((SB_WIDTH + SW_WIDTH, d), lambda b, i: (0, 0)),
            pl.BlockSpec((1, d), lambda b, i: (0, 0)),
        ],
        out_specs=pl.BlockSpec((None, OUT_ROW_BLOCK, d), lambda b, i: (b, i, 0)),
        out_shape=jax.ShapeDtypeStruct((bsz, s, d), _F32),
        compiler_params=pltpu.CompilerParams(
            dimension_semantics=("parallel", "parallel"), vmem_limit_bytes=VMEM_LIMIT),
        name="out_proj",
    )(x, y_sb, y_sw, mod3, w_out_bf16, final_g.reshape(1, d))


def kernel(x, c, w_ada, b_ada, norm_g, w_in, sinks, w_out, final_g):
    assert w_ada.shape[0] == 1, "the output projection kernel fuses the final norm: single layer only"
    bsz = x.shape[0]
    slopes = jnp.asarray([2.0 ** (-8.0 * (h + 1) / SW_HEADS) for h in range(SW_HEADS)], dtype=_F32)
    mod3 = _adaln(c, w_ada[0], b_ada[0]).reshape(bsz, 1, 3 * D_MODEL)
    proj = _inproj(x, mod3, norm_g[0], w_in[0].astype(_BF16))
    y_sb = _sb_attention(proj)
    y_sw = _sw_attention(proj, sinks[0], slopes)
    return _outproj(x, y_sb, y_sw, mod3, w_out[0].astype(_BF16), final_g)
```

```python
import jax
import jax.numpy as jnp
from jax import lax
from jax.experimental import pallas as pl
from jax.experimental.pallas import tpu as pltpu

D_MODEL = 1024
HEAD_DIM = 64
SB_HEADS = 8
SW_HEADS = 8
SW_KV_HEADS = 2
SB_WIDTH = SB_HEADS * HEAD_DIM
SW_WIDTH = SW_HEADS * HEAD_DIM
SW_KV_WIDTH = SW_KV_HEADS * HEAD_DIM
IN_WIDTH = 4 * SB_WIDTH + 2 * SW_WIDTH + 2 * SW_KV_WIDTH
WINDOW = 128
RMS_EPS = 1e-6
NEG_INF = -1e30

LANES = 128
PAIRS = SB_WIDTH // LANES
SB_Q_BLK, SB_K_BLK, SB_V_BLK, SB_G_BLK = 0, PAIRS, 2 * PAIRS, 3 * PAIRS
SW_Q_BLK = 4 * PAIRS
SW_K_BLK = SW_Q_BLK + PAIRS
SW_V_BLK = SW_K_BLK + 1
SW_G_BLK = SW_V_BLK + 1

ROW_BLOCK = 512
OUT_ROW_BLOCK = 1024
PROJ_COLS = 256
SB_TILE = 256
SB_GROUP = 4
SW_PREP_ROWS = 512
SW_GROUP = 8
SB_SKIP = 105.0
VMEM_LIMIT = 48 * 1024 * 1024

_F32 = jnp.float32
_BF16 = jnp.bfloat16


def _silu(v):
    return v / (1.0 + jnp.exp(-v))


def _adaln_kernel(c_ref, w_ref, b_ref, o_ref):
    cond = _silu(c_ref[...])
    o_ref[...] = jnp.dot(cond, w_ref[...], preferred_element_type=_F32) + b_ref[...]


def _adaln(c, w_ada, b_ada):
    bsz, d = c.shape
    n = w_ada.shape[1]
    bn = 512
    return pl.pallas_call(
        _adaln_kernel,
        grid=(n // bn,),
        in_specs=[
            pl.BlockSpec((bsz, d), lambda j: (0, 0)),
            pl.BlockSpec((d, bn), lambda j: (0, j)),
            pl.BlockSpec((1, bn), lambda j: (0, j)),
        ],
        out_specs=pl.BlockSpec((bsz, bn), lambda j: (0, j)),
        out_shape=jax.ShapeDtypeStruct((bsz, n), _F32),
        name="adaln_mod",
    )(c, w_ada, b_ada.reshape(1, n))


def _inproj_kernel(x_ref, mod_ref, g_ref, w_ref, o_ref):
    x = x_ref[...]
    r = lax.rsqrt(jnp.mean(x * x, axis=-1, keepdims=True) + RMS_EPS)
    shift = mod_ref[:, 0:D_MODEL]
    scale = mod_ref[:, D_MODEL:2 * D_MODEL]
    h = ((x * r) * g_ref[...]) * (1.0 + scale) + shift
    hb = h.astype(_BF16)
    for n0 in range(0, IN_WIDTH, PROJ_COLS):
        o_ref[:, n0:n0 + PROJ_COLS] = jnp.dot(
            hb, w_ref[:, n0:n0 + PROJ_COLS], preferred_element_type=_F32).astype(_BF16)


def _inproj(x, mod3, norm_g, w_in_bf16):
    bsz, s, d = x.shape
    return pl.pallas_call(
        _inproj_kernel,
        grid=(bsz, s // ROW_BLOCK),
        in_specs=[
            pl.BlockSpec((None, ROW_BLOCK, d), lambda b, i: (b, i, 0)),
            pl.BlockSpec((None, 1, 3 * d), lambda b, i: (b, 0, 0)),
            pl.BlockSpec((1, d), lambda b, i: (0, 0)),
            pl.BlockSpec((d, IN_WIDTH), lambda b, i: (0, 0)),
        ],
        out_specs=pl.BlockSpec((None, ROW_BLOCK, IN_WIDTH), lambda b, i: (b, i, 0)),
        out_shape=jax.ShapeDtypeStruct((bsz, s, IN_WIDTH), _BF16),
        compiler_params=pltpu.CompilerParams(
            dimension_semantics=("parallel", "parallel"), vmem_limit_bytes=VMEM_LIMIT),
        name="in_proj",
    )(x, mod3, norm_g.reshape(1, d), w_in_bf16)


def _sb_kernel(q_ref, k_ref, v_ref, g_ref, o_ref, u_ref, q2_ref, acc_ref, carry_ref):
    t = SB_TILE
    seq = q_ref.shape[0]
    row = lax.broadcasted_iota(jnp.int32, (t, t), 0)
    col = lax.broadcasted_iota(jnp.int32, (t, t), 1)
    u_ref[...] = jnp.where(row > col, 1.0, 0.0).astype(_BF16)
    row2 = lax.broadcasted_iota(jnp.int32, (2 * t, t), 0) & (t - 1)
    col2 = lax.broadcasted_iota(jnp.int32, (2 * t, t), 1)
    causal = col2 < row2
    head_a = lax.broadcasted_iota(jnp.int32, (t, LANES), 1) < HEAD_DIM

    def tile(u, kj, diag):
        ks = kj * t if isinstance(kj, int) else pl.multiple_of(kj * t, t)
        k = k_ref[pl.ds(ks, t), :]
        v = v_ref[pl.ds(ks, t), :]
        z = lax.dot_general(q2_ref[u], k, (((1,), (1,)), ((), ())), preferred_element_type=_F32)
        zpos = jnp.maximum(z, 0.0)
        zneg = z - zpos
        lse = jnp.log(1.0 + jnp.exp(zneg - zpos))
        p = zpos + lse
        if diag:
            p = jnp.where(causal, p, 0.0)
        later = jnp.dot(p.astype(_BF16), u_ref[...], preferred_element_type=_F32)
        w = jnp.exp((zneg - lse) - later)
        if diag:
            w = jnp.where(causal, w, 0.0)
        o = jnp.dot(w.astype(_BF16), v, preferred_element_type=_F32)
        return o, jnp.sum(p, axis=1, keepdims=True)

    def load_q(u, qs):
        q = q_ref[pl.ds(qs, t), :] * jnp.asarray(HEAD_DIM ** -0.5, _BF16)
        zero = jnp.zeros_like(q)
        q2_ref[u] = jnp.concatenate([jnp.where(head_a, q, zero), jnp.where(head_a, zero, q)], axis=0)

    def finish(u, qs):
        acc = acc_ref[u]
        y = jnp.where(head_a, acc[0:t], acc[t:2 * t])
        g = g_ref[pl.ds(qs, t), :].astype(_F32)
        o_ref[pl.ds(qs, t), :] = (y * _silu(g)).astype(_BF16)

    def q_group(first_block, has_prev):
        starts, carry_max = [], []
        for u in range(SB_GROUP):
            qi = first_block + u
            qs = qi * t if isinstance(qi, int) else pl.multiple_of(qi * t, t)
            starts.append(qs)
            load_q(u, qs)
            o0, rs0 = tile(u, qi, True)
            if has_prev or u > 0:
                o1, rs1 = tile(u, qi - 1, False)
                c1 = -rs0
                acc_ref[u] = o0 + jnp.exp(c1) * o1
                carry = c1 - rs1
            else:
                acc_ref[u] = o0
                carry = -rs0
            carry_ref[u] = carry
            carry_max.append(jnp.max(carry))
        for u in range(SB_GROUP):
            qi = first_block + u

            def more(state):
                kj, cmax = state
                return jnp.logical_and(kj >= 0, cmax > -SB_SKIP)

            def k_block(state, u=u):
                kj, _ = state
                o, rs = tile(u, kj, False)
                c = carry_ref[u]
                acc_ref[u] += jnp.exp(c) * o
                c = c - rs
                carry_ref[u] = c
                return kj - 1, jnp.max(c)

            lax.while_loop(more, k_block, (qi - 2, carry_max[u]))
        for u in range(SB_GROUP):
            finish(u, starts[u])

    q_group(0, False)

    def body(i, _):
        q_group(i * SB_GROUP, True)
        return 0

    lax.fori_loop(1, seq // (t * SB_GROUP), body, 0)


def _sb_attention(proj):
    bsz, s, _ = proj.shape
    t = SB_TILE

    def spec(blk):
        return pl.BlockSpec((None, s, LANES), lambda b, j, blk=blk: (b, 0, blk + j))

    return pl.pallas_call(
        _sb_kernel,
        grid=(bsz, PAIRS),
        in_specs=[spec(SB_Q_BLK), spec(SB_K_BLK), spec(SB_V_BLK), spec(SB_G_BLK)],
        out_specs=pl.BlockSpec((None, s, LANES), lambda b, j: (b, 0, j)),
        out_shape=jax.ShapeDtypeStruct((bsz, s, SB_WIDTH), _BF16),
        scratch_shapes=[
            pltpu.VMEM((t, t), _BF16),
            pltpu.VMEM((SB_GROUP, 2 * t, LANES), _BF16),
            pltpu.VMEM((SB_GROUP, 2 * t, LANES), _F32),
            pltpu.VMEM((SB_GROUP, 2 * t, 1), _F32),
        ],
        compiler_params=pltpu.CompilerParams(
            dimension_semantics=("parallel", "parallel"), vmem_limit_bytes=VMEM_LIMIT),
        name="sb_attention",
    )(proj, proj, proj, proj)


def _sw_kernel(sinks_ref, slopes_ref, q_ref, k_ref, v_ref, g_ref, o_ref, bias_ref, ksel_ref, vext_ref):
    w = WINDOW
    seq = q_ref.shape[0]
    j = pl.program_id(1)
    kvh = j // (PAIRS // SW_KV_HEADS)
    lane = lax.broadcasted_iota(jnp.int32, (w, LANES), 1)
    head_a = lane < HEAD_DIM

    chunk = SW_PREP_ROWS
    low = lax.broadcasted_iota(jnp.int32, (chunk, LANES), 1) < HEAD_DIM

    @pl.when(j == 0)
    def _():
        def prep(i, _):
            rs = pl.multiple_of(i * chunk, chunk)
            kc = k_ref[pl.ds(rs, chunk), :]
            vc = v_ref[pl.ds(rs, chunk), :]
            kr = pltpu.roll(kc, HEAD_DIM, 1)
            vr = pltpu.roll(vc, HEAD_DIM, 1)
            ones = jnp.ones((chunk, LANES), _BF16)
            ksel_ref[0, pl.ds(rs, chunk), :] = jnp.where(low, kc, kr)
            ksel_ref[1, pl.ds(rs, chunk), :] = jnp.where(low, kr, kc)
            vext_ref[0, pl.ds(rs, chunk), 0:LANES] = jnp.where(low, vc, vr)
            vext_ref[1, pl.ds(rs, chunk), 0:LANES] = jnp.where(low, vr, vc)
            vext_ref[0, pl.ds(rs, chunk), LANES:2 * LANES] = ones
            vext_ref[1, pl.ds(rs, chunk), LANES:2 * LANES] = ones
            return 0

        lax.fori_loop(0, seq // chunk, prep, 0)

    row = lax.broadcasted_iota(jnp.int32, (2 * w, 2 * w), 0)
    col = lax.broadcasted_iota(jnp.int32, (2 * w, 2 * w), 1)
    rel = w + (row & (w - 1)) - col
    slope = jnp.where(row < w, slopes_ref[2 * j], slopes_ref[2 * j + 1])
    valid = (rel >= 0) & (rel < w)
    bias_ref[...] = jnp.where(valid, -slope * rel.astype(_F32), NEG_INF)
    rowc = lax.broadcasted_iota(jnp.int32, (2 * w, 1), 0)
    sink = jnp.where(rowc < w, sinks_ref[2 * j], sinks_ref[2 * j + 1])

    def q_block(n, first):
        qs = n * w if isinstance(n, int) else pl.multiple_of(n * w, w)
        q = q_ref[pl.ds(qs, w), :] * jnp.asarray(HEAD_DIM ** -0.5, _BF16)
        zero = jnp.zeros_like(q)
        q2 = jnp.concatenate([jnp.where(head_a, q, zero), jnp.where(head_a, zero, q)], axis=0)
        if first:
            k = ksel_ref[kvh, 0:w, :]
            v = vext_ref[kvh, 0:w, :]
            bias = bias_ref[:, w:2 * w]
        else:
            ks = qs - w if isinstance(n, int) else pl.multiple_of(qs - w, w)
            k = ksel_ref[kvh, pl.ds(ks, 2 * w), :]
            v = vext_ref[kvh, pl.ds(ks, 2 * w), :]
            bias = bias_ref[...]
        logits = lax.dot_general(q2, k, (((1,), (1,)), ((), ())), preferred_element_type=_F32) + bias
        mx = jnp.maximum(jnp.max(logits, axis=1, keepdims=True), sink)
        p = jnp.exp(logits - mx)
        oe = jnp.dot(p.astype(_BF16), v, preferred_element_type=_F32)
        y2 = oe[:, 0:LANES] / (oe[:, LANES:2 * LANES] + jnp.exp(sink - mx))
        y = jnp.where(head_a, y2[0:w], y2[w:2 * w])
        g = g_ref[pl.ds(qs, w), :].astype(_F32)
        o_ref[pl.ds(qs, w), :] = (y * _silu(g)).astype(_BF16)

    for n in range(SW_GROUP):
        q_block(n, n == 0)

    def body(i, _):
        for u in range(SW_GROUP):
            q_block(i * SW_GROUP + u, False)
        return 0

    lax.fori_loop(1, seq // (w * SW_GROUP), body, 0)


def _sw_attention(proj, sinks, slopes):
    bsz, s, _ = proj.shape

    def spec(blk, per_pair):
        if per_pair:
            return pl.BlockSpec((None, s, LANES), lambda b, j, blk=blk: (b, 0, blk + j))
        return pl.BlockSpec((None, s, LANES), lambda b, j, blk=blk: (b, 0, blk))

    smem = pl.BlockSpec(memory_space=pltpu.SMEM)
    return pl.pallas_call(
        _sw_kernel,
        grid=(bsz, PAIRS),
        in_specs=[smem, smem, spec(SW_Q_BLK, True), spec(SW_K_BLK, False), spec(SW_V_BLK, False),
                  spec(SW_G_BLK, True)],
        out_specs=pl.BlockSpec((None, s, LANES), lambda b, j: (b, 0, j)),
        out_shape=jax.ShapeDtypeStruct((bsz, s, SW_WIDTH), _BF16),
        scratch_shapes=[
            pltpu.VMEM((2 * WINDOW, 2 * WINDOW), _F32),
            pltpu.VMEM((SW_KV_HEADS, s, LANES), _BF16),
            pltpu.VMEM((SW_KV_HEADS, s, 2 * LANES), _BF16),
        ],
        compiler_params=pltpu.CompilerParams(
            dimension_semantics=("parallel", "arbitrary"), vmem_limit_bytes=VMEM_LIMIT),
        name="sw_attention",
    )(sinks, slopes, proj, proj, proj, proj)


def _outproj_kernel(x_ref, ysb_ref, ysw_ref, mod_ref, w_ref, fg_ref, o_ref):
    y = jnp.dot(ysb_ref[...], w_ref[0:SB_WIDTH, :], preferred_element_type=_F32)
    y = y + jnp.dot(ysw_ref[...], w_ref[SB_WIDTH:SB_WIDTH + SW_WIDTH, :], preferred_element_type=_F32)
    gate = mod_ref[:, 2 * D_MODEL:3 * D_MODEL]
    xn = x_ref[...] + gate * y
    r = lax.rsqrt(jnp.mean(xn * xn, axis=-1, keepdims=True) + RMS_EPS)
    o_ref[...] = (xn * r) * fg_ref[...]


def _outproj(x, y_sb, y_sw, mod3, w_out_bf16, final_g):
    bsz, s, d = x.shape
    return pl.pallas_call(
        _outproj_kernel,
        grid=(bsz, s // OUT_ROW_BLOCK),
        in_specs=[
            pl.BlockSpec((None, OUT_ROW_BLOCK, d), lambda b, i: (b, i, 0)),
            pl.BlockSpec((None, OUT_ROW_BLOCK, SB_WIDTH), lambda b, i: (b, i, 0)),
            pl.BlockSpec((None, OUT_ROW_BLOCK, SW_WIDTH), lambda b, i: (b, i, 0)),
            pl.BlockSpec((None, 1, 3 * d), lambda b, i: (b, 0, 0)),
            pl.BlockSpec((SB_WIDTH + SW_WIDTH, d), lambda b, i: (0, 0)),
            pl.BlockSpec((1, d), lambda b, i: (0, 0)),
        ],
        out_specs=pl.BlockSpec((None, OUT_ROW_BLOCK, d), lambda b, i: (b, i, 0)),
        out_shape=jax.ShapeDtypeStruct((bsz, s, d), _F32),
        compiler_params=pltpu.CompilerParams(
            dimension_semantics=("parallel", "parallel"), vmem_limit_bytes=VMEM_LIMIT),
        name="out_proj",
    )(x, y_sb, y_sw, mod3, w_out_bf16, final_g.reshape(1, d))


def kernel(x, c, w_ada, b_ada, norm_g, w_in, sinks, w_out, final_g):
    assert w_ada.shape[0] == 1, "the output projection kernel fuses the final norm: single layer only"
    bsz = x.shape[0]
    slopes = jnp.asarray([2.0 ** (-8.0 * (h + 1) / SW_HEADS) for h in range(SW_HEADS)], dtype=_F32)
    mod3 = _adaln(c, w_ada[0], b_ada[0]).reshape(bsz, 1, 3 * D_MODEL)
    proj = _inproj(x, mod3, norm_g[0], w_in[0].astype(_BF16))
    y_sb = _sb_attention(proj)
    y_sw = _sw_attention(proj, sinks[0], slopes)
    return _outproj(x, y_sb, y_sw, mod3, w_out[0].astype(_BF16), final_g)
```

```python
import jax
import jax.numpy as jnp
from jax import lax
from jax.experimental import pallas as pl
from jax.experimental.pallas import tpu as pltpu

D_MODEL = 1024
HEAD_DIM = 64
SB_HEADS = 8
SW_HEADS = 8
SW_KV_HEADS = 2
SB_WIDTH = SB_HEADS * HEAD_DIM
SW_WIDTH = SW_HEADS * HEAD_DIM
SW_KV_WIDTH = SW_KV_HEADS * HEAD_DIM
IN_WIDTH = 4 * SB_WIDTH + 2 * SW_WIDTH + 2 * SW_KV_WIDTH
WINDOW = 128
RMS_EPS = 1e-6
NEG_INF = -1e30

LANES = 128
PAIRS = SB_WIDTH // LANES
SB_Q_BLK, SB_K_BLK, SB_V_BLK, SB_G_BLK = 0, PAIRS, 2 * PAIRS, 3 * PAIRS
SW_Q_BLK = 4 * PAIRS
SW_K_BLK = SW_Q_BLK + PAIRS
SW_V_BLK = SW_K_BLK + 1
SW_G_BLK = SW_V_BLK + 1

ROW_BLOCK = 1024
OUT_ROW_BLOCK = 1024
PROJ_COLS = 256
SB_TILE = 256
SB_GROUP = 4
SW_PREP_ROWS = 512
SW_GROUP = 8
SB_SKIP = 105.0
VMEM_LIMIT = 48 * 1024 * 1024

_F32 = jnp.float32
_BF16 = jnp.bfloat16


def _silu(v):
    return v / (1.0 + jnp.exp(-v))


def _adaln_kernel(c_ref, w_ref, b_ref, o_ref):
    cond = _silu(c_ref[...])
    o_ref[...] = jnp.dot(cond, w_ref[...], preferred_element_type=_F32) + b_ref[...]


def _adaln(c, w_ada, b_ada):
    bsz, d = c.shape
    n = w_ada.shape[1]
    bn = 512
    return pl.pallas_call(
        _adaln_kernel,
        grid=(n // bn,),
        in_specs=[
            pl.BlockSpec((bsz, d), lambda j: (0, 0)),
            pl.BlockSpec((d, bn), lambda j: (0, j)),
            pl.BlockSpec((1, bn), lambda j: (0, j)),
        ],
        out_specs=pl.BlockSpec((bsz, bn), lambda j: (0, j)),
        out_shape=jax.ShapeDtypeStruct((bsz, n), _F32),
        name="adaln_mod",
    )(c, w_ada, b_ada.reshape(1, n))


def _inproj_kernel(x_ref, mod_ref, g_ref, w_ref, o_ref):
    x = x_ref[...]
    r = lax.rsqrt(jnp.mean(x * x, axis=-1, keepdims=True) + RMS_EPS)
    shift = mod_ref[:, 0:D_MODEL]
    scale = mod_ref[:, D_MODEL:2 * D_MODEL]
    h = ((x * r) * g_ref[...]) * (1.0 + scale) + shift
    hb = h.astype(_BF16)
    for n0 in range(0, IN_WIDTH, PROJ_COLS):
        o_ref[:, n0:n0 + PROJ_COLS] = jnp.dot(
            hb, w_ref[:, n0:n0 + PROJ_COLS], preferred_element_type=_F32).astype(_BF16)


def _inproj(x, mod3, norm_g, w_in_bf16):
    bsz, s, d = x.shape
    return pl.pallas_call(
        _inproj_kernel,
        grid=(bsz, s // ROW_BLOCK),
        in_specs=[
            pl.BlockSpec((None, ROW_BLOCK, d), lambda b, i: (b, i, 0)),
            pl.BlockSpec((None, 1, 3 * d), lambda b, i: (b, 0, 0)),
            pl.BlockSpec((1, d), lambda b, i: (0, 0)),
            pl.BlockSpec((d, IN_WIDTH), lambda b, i: (0, 0)),
        ],
        out_specs=pl.BlockSpec((None, ROW_BLOCK, IN_WIDTH), lambda b, i: (b, i, 0)),
        out_shape=jax.ShapeDtypeStruct((bsz, s, IN_WIDTH), _BF16),
        compiler_params=pltpu.CompilerParams(
            dimension_semantics=("parallel", "parallel"), vmem_limit_bytes=VMEM_LIMIT),
        name="in_proj",
    )(x, mod3, norm_g.reshape(1, d), w_in_bf16)


def _sb_kernel(q_ref, k_ref, v_ref, g_ref, o_ref, u_ref, q2_ref, acc_ref, carry_ref):
    t = SB_TILE
    seq = q_ref.shape[0]
    row = lax.broadcasted_iota(jnp.int32, (t, t), 0)
    col = lax.broadcasted_iota(jnp.int32, (t, t), 1)
    u_ref[...] = jnp.where(row > col, 1.0, 0.0).astype(_BF16)
    row2 = lax.broadcasted_iota(jnp.int32, (2 * t, t), 0) & (t - 1)
    col2 = lax.broadcasted_iota(jnp.int32, (2 * t, t), 1)
    causal = col2 < row2
    head_a = lax.broadcasted_iota(jnp.int32, (t, LANES), 1) < HEAD_DIM

    def softplus_parts(z):
        zpos = jnp.maximum(z, 0.0)
        zneg = z - zpos
        lse = jnp.log(1.0 + jnp.exp(zneg - zpos))
        return zpos + lse, zneg - lse

    def scores(u, ks, width):
        k = k_ref[pl.ds(ks, width), :]
        return lax.dot_general(q2_ref[u], k, (((1,), (1,)), ((), ())), preferred_element_type=_F32)

    def later_sum(p):
        return jnp.dot(p.astype(_BF16), u_ref[...], preferred_element_type=_F32)

    def diagonal_only(u, qs):
        p, logb = softplus_parts(scores(u, qs, t))
        p = jnp.where(causal, p, 0.0)
        w = jnp.where(causal, jnp.exp(logb - later_sum(p)), 0.0)
        o = jnp.dot(w.astype(_BF16), v_ref[pl.ds(qs, t), :], preferred_element_type=_F32)
        return o, -jnp.sum(p, axis=1, keepdims=True)

    def window(u, qs):
        ws = qs - t if isinstance(qs, int) else pl.multiple_of(qs - t, t)
        p, logb = softplus_parts(scores(u, ws, 2 * t))
        p_prev, p_diag = p[:, 0:t], jnp.where(causal, p[:, t:2 * t], 0.0)
        rs_diag = jnp.sum(p_diag, axis=1, keepdims=True)
        w_diag = jnp.where(causal, jnp.exp(logb[:, t:2 * t] - later_sum(p_diag)), 0.0)
        w_prev = jnp.exp((logb[:, 0:t] - rs_diag) - later_sum(p_prev))
        w = jnp.concatenate([w_prev.astype(_BF16), w_diag.astype(_BF16)], axis=1)
        o = jnp.dot(w, v_ref[pl.ds(ws, 2 * t), :], preferred_element_type=_F32)
        return o, -(rs_diag + jnp.sum(p_prev, axis=1, keepdims=True))

    def tile(u, kj):
        ks = pl.multiple_of(kj * t, t)
        p, logb = softplus_parts(scores(u, ks, t))
        w = jnp.exp(logb - later_sum(p))
        o = jnp.dot(w.astype(_BF16), v_ref[pl.ds(ks, t), :], preferred_element_type=_F32)
        return o, jnp.sum(p, axis=1, keepdims=True)

    def load_q(u, qs):
        q = q_ref[pl.ds(qs, t), :] * jnp.asarray(HEAD_DIM ** -0.5, _BF16)
        zero = jnp.zeros_like(q)
        q2_ref[u] = jnp.concatenate([jnp.where(head_a, q, zero), jnp.where(head_a, zero, q)], axis=0)

    def finish(u, qs):
        acc = acc_ref[u]
        y = jnp.where(head_a, acc[0:t], acc[t:2 * t])
        g = g_ref[pl.ds(qs, t), :].astype(_F32)
        o_ref[pl.ds(qs, t), :] = (y * _silu(g)).astype(_BF16)

    def q_group(first_block, has_prev):
        starts, carry_max = [], []
        for u in range(SB_GROUP):
            qi = first_block + u
            qs = qi * t if isinstance(qi, int) else pl.multiple_of(qi * t, t)
            starts.append(qs)
            load_q(u, qs)
            o, carry = window(u, qs) if (has_prev or u > 0) else diagonal_only(u, qs)
            acc_ref[u] = o
            carry_ref[u] = carry
            carry_max.append(jnp.max(carry))
        for u in range(SB_GROUP):
            qi = first_block + u

            def more(state):
                kj, cmax = state
                return jnp.logical_and(kj >= 0, cmax > -SB_SKIP)

            def k_block(state, u=u):
                kj, _ = state
                o, rs = tile(u, kj)
                c = carry_ref[u]
                acc_ref[u] += jnp.exp(c) * o
                c = c - rs
                carry_ref[u] = c
                return kj - 1, jnp.max(c)

            lax.while_loop(more, k_block, (qi - 2, carry_max[u]))
        for u in range(SB_GROUP):
            finish(u, starts[u])

    q_group(0, False)

    def body(i, _):
        q_group(i * SB_GROUP, True)
        return 0

    lax.fori_loop(1, seq // (t * SB_GROUP), body, 0)


def _sb_attention(proj):
    bsz, s, _ = proj.shape
    t = SB_TILE

    def spec(blk):
        return pl.BlockSpec((None, s, LANES), lambda b, j, blk=blk: (b, 0, blk + j))

    return pl.pallas_call(
        _sb_kernel,
        grid=(bsz, PAIRS),
        in_specs=[spec(SB_Q_BLK), spec(SB_K_BLK), spec(SB_V_BLK), spec(SB_G_BLK)],
        out_specs=pl.BlockSpec((None, s, LANES), lambda b, j: (b, 0, j)),
        out_shape=jax.ShapeDtypeStruct((bsz, s, SB_WIDTH), _BF16),
        scratch_shapes=[
            pltpu.VMEM((t, t), _BF16),
            pltpu.VMEM((SB_GROUP, 2 * t, LANES), _BF16),
            pltpu.VMEM((SB_GROUP, 2 * t, LANES), _F32),
            pltpu.VMEM((SB_GROUP, 2 * t, 1), _F32),
        ],
        compiler_params=pltpu.CompilerParams(
            dimension_semantics=("parallel", "parallel"), vmem_limit_bytes=VMEM_LIMIT),
        name="sb_attention",
    )(proj, proj, proj, proj)


def _sw_kernel(sinks_ref, slopes_ref, q_ref, k_ref, v_ref, g_ref, o_ref, bias_ref, ksel_ref, vext_ref):
    w = WINDOW
    seq = q_ref.shape[0]
    j = pl.program_id(1)
    kvh = j // (PAIRS // SW_KV_HEADS)
    lane = lax.broadcasted_iota(jnp.int32, (w, LANES), 1)
    head_a = lane < HEAD_DIM

    chunk = SW_PREP_ROWS
    low = lax.broadcasted_iota(jnp.int32, (chunk, LANES), 1) < HEAD_DIM

    @pl.when(j == 0)
    def _():
        def prep(i, _):
            rs = pl.multiple_of(i * chunk, chunk)
            kc = k_ref[pl.ds(rs, chunk), :]
            vc = v_ref[pl.ds(rs, chunk), :]
            kr = pltpu.roll(kc, HEAD_DIM, 1)
            vr = pltpu.roll(vc, HEAD_DIM, 1)
            ones = jnp.ones((chunk, LANES), _BF16)
            ksel_ref[0, pl.ds(rs, chunk), :] = jnp.where(low, kc, kr)
            ksel_ref[1, pl.ds(rs, chunk), :] = jnp.where(low, kr, kc)
            vext_ref[0, pl.ds(rs, chunk), 0:LANES] = jnp.where(low, vc, vr)
            vext_ref[1, pl.ds(rs, chunk), 0:LANES] = jnp.where(low, vr, vc)
            vext_ref[0, pl.ds(rs, chunk), LANES:2 * LANES] = ones
            vext_ref[1, pl.ds(rs, chunk), LANES:2 * LANES] = ones
            return 0

        lax.fori_loop(0, seq // chunk, prep, 0)

    row = lax.broadcasted_iota(jnp.int32, (2 * w, 2 * w), 0)
    col = lax.broadcasted_iota(jnp.int32, (2 * w, 2 * w), 1)
    rel = w + (row & (w - 1)) - col
    slope = jnp.where(row < w, slopes_ref[2 * j], slopes_ref[2 * j + 1])
    valid = (rel >= 0) & (rel < w)
    bias_ref[...] = jnp.where(valid, -slope * rel.astype(_F32), NEG_INF)
    rowc = lax.broadcasted_iota(jnp.int32, (2 * w, 1), 0)
    sink = jnp.where(rowc < w, sinks_ref[2 * j], sinks_ref[2 * j + 1])

    def q_block(n, first):
        qs = n * w if isinstance(n, int) else pl.multiple_of(n * w, w)
        q = q_ref[pl.ds(qs, w), :] * jnp.asarray(HEAD_DIM ** -0.5, _BF16)
        zero = jnp.zeros_like(q)
        q2 = jnp.concatenate([jnp.where(head_a, q, zero), jnp.where(head_a, zero, q)], axis=0)
        if first:
            k = ksel_ref[kvh, 0:w, :]
            v = vext_ref[kvh, 0:w, :]
            bias = bias_ref[:, w:2 * w]
        else:
            ks = qs - w if isinstance(n, int) else pl.multiple_of(qs - w, w)
            k = ksel_ref[kvh, pl.ds(ks, 2 * w), :]
            v = vext_ref[kvh, pl.ds(ks, 2 * w), :]
            bias = bias_ref[...]
        logits = lax.dot_general(q2, k, (((1,), (1,)), ((), ())), preferred_element_type=_F32) + bias
        mx = jnp.maximum(jnp.max(logits, axis=1, keepdims=True), sink)
        p = jnp.exp(logits - mx)
        oe = jnp.dot(p.astype(_BF16), v, preferred_element_type=_F32)
        y2 = oe[:, 0:LANES] / (oe[:, LANES:2 * LANES] + jnp.exp(sink - mx))
        y = jnp.where(head_a, y2[0:w], y2[w:2 * w])
        g = g_ref[pl.ds(qs, w), :].astype(_F32)
        o_ref[pl.ds(qs, w), :] = (y * _silu(g)).astype(_BF16)

    for n in range(SW_GROUP):
        q_block(n, n == 0)

    def body(i, _):
        for u in range(SW_GROUP):
            q_block(i * SW_GROUP + u, False)
        return 0

    lax.fori_loop(1, seq // (w * SW_GROUP), body, 0)


def _sw_attention(proj, sinks, slopes):
    bsz, s, _ = proj.shape

    def spec(blk, per_pair):
        if per_pair:
            return pl.BlockSpec((None, s, LANES), lambda b, j, blk=blk: (b, 0, blk + j))
        return pl.BlockSpec((None, s, LANES), lambda b, j, blk=blk: (b, 0, blk))

    smem = pl.BlockSpec(memory_space=pltpu.SMEM)
    return pl.pallas_call(
        _sw_kernel,
        grid=(bsz, PAIRS),
        in_specs=[smem, smem, spec(SW_Q_BLK, True), spec(SW_K_BLK, False), spec(SW_V_BLK, False),
                  spec(SW_G_BLK, True)],
        out_specs=pl.BlockSpec((None, s, LANES), lambda b, j: (b, 0, j)),
        out_shape=jax.ShapeDtypeStruct((bsz, s, SW_WIDTH), _BF16),
        scratch_shapes=[
            pltpu.VMEM((2 * WINDOW, 2 * WINDOW), _F32),
            pltpu.VMEM((SW_KV_HEADS, s, LANES), _BF16),
            pltpu.VMEM((SW_KV_HEADS, s, 2 * LANES), _BF16),
        ],
        compiler_params=pltpu.CompilerParams(
            dimension_semantics=("parallel", "arbitrary"), vmem_limit_bytes=VMEM_LIMIT),
        name="sw_attention",
    )(sinks, slopes, proj, proj, proj, proj)


def _outproj_kernel(x_ref, ysb_ref, ysw_ref, mod_ref, w_ref, fg_ref, o_ref):
    y = jnp.dot(ysb_ref[...], w_ref[0:SB_WIDTH, :], preferred_element_type=_F32)
    y = y + jnp.dot(ysw_ref[...], w_ref[SB_WIDTH:SB_WIDTH + SW_WIDTH, :], preferred_element_type=_F32)
    gate = mod_ref[:, 2 * D_MODEL:3 * D_MODEL]
    xn = x_ref[...] + gate * y
    r = lax.rsqrt(jnp.mean(xn * xn, axis=-1, keepdims=True) + RMS_EPS)
    o_ref[...] = (xn * r) * fg_ref[...]


def _outproj(x, y_sb, y_sw, mod3, w_out_bf16, final_g):
    bsz, s, d = x.shape
    return pl.pallas_call(
        _outproj_kernel,
        grid=(bsz, s // OUT_ROW_BLOCK),
        in_specs=[
            pl.BlockSpec((None, OUT_ROW_BLOCK, d), lambda b, i: (b, i, 0)),
            pl.BlockSpec((None, OUT_ROW_BLOCK, SB_WIDTH), lambda b, i: (b, i, 0)),
            pl.BlockSpec((None, OUT_ROW_BLOCK, SW_WIDTH), lambda b, i: (b, i, 0)),
            pl.BlockSpec((None, 1, 3 * d), lambda b, i: (b, 0, 0)),
            pl.BlockSpec((SB_WIDTH + SW_WIDTH, d), lambda b, i: (0, 0)),
            pl.BlockSpec((1, d), lambda b, i: (0, 0)),
        ],
        out_specs=pl.BlockSpec((None, OUT_ROW_BLOCK, d), lambda b, i: (b, i, 0)),
        out_shape=jax.ShapeDtypeStruct((bsz, s, d), _F32),
        compiler_params=pltpu.CompilerParams(
            dimension_semantics=("parallel", "parallel"), vmem_limit_bytes=VMEM_LIMIT),
        name="out_proj",
    )(x, y_sb, y_sw, mod3, w_out_bf16, final_g.reshape(1, d))


def kernel(x, c, w_ada, b_ada, norm_g, w_in, sinks, w_out, final_g):
    assert w_ada.shape[0] == 1, "the output projection kernel fuses the final norm: single layer only"
    bsz = x.shape[0]
    slopes = jnp.asarray([2.0 ** (-8.0 * (h + 1) / SW_HEADS) for h in range(SW_HEADS)], dtype=_F32)
    mod3 = _adaln(c, w_ada[0], b_ada[0]).reshape(bsz, 1, 3 * D_MODEL)
    proj = _inproj(x, mod3, norm_g[0], w_in[0].astype(_BF16))
    y_sb = _sb_attention(proj)
    y_sw = _sw_attention(proj, sinks[0], slopes)
    return _outproj(x, y_sb, y_sw, mod3, w_out[0].astype(_BF16), final_g)
```

```python
import jax
import jax.numpy as jnp
from jax import lax
from jax.experimental import pallas as pl
from jax.experimental.pallas import tpu as pltpu

D_MODEL = 1024
HEAD_DIM = 64
SB_HEADS = 8
SW_HEADS = 8
SW_KV_HEADS = 2
SB_WIDTH = SB_HEADS * HEAD_DIM
SW_WIDTH = SW_HEADS * HEAD_DIM
SW_KV_WIDTH = SW_KV_HEADS * HEAD_DIM
IN_WIDTH = 4 * SB_WIDTH + 2 * SW_WIDTH + 2 * SW_KV_WIDTH
WINDOW = 128
RMS_EPS = 1e-6
NEG_INF = -1e30

LANES = 128
PAIRS = SB_WIDTH // LANES
SB_Q_BLK, SB_K_BLK, SB_V_BLK, SB_G_BLK = 0, PAIRS, 2 * PAIRS, 3 * PAIRS
SW_Q_BLK = 4 * PAIRS
SW_K_BLK = SW_Q_BLK + PAIRS
SW_V_BLK = SW_K_BLK + 1
SW_G_BLK = SW_V_BLK + 1

ROW_BLOCK = 1024
OUT_ROW_BLOCK = 1024
OUT_ROW_CHUNK = 256
PROJ_COLS = 256
SB_TILE = 256
SB_GROUP = 4
SW_PREP_ROWS = 512
SW_GROUP = 8
SB_SKIP = 105.0
VMEM_LIMIT = 48 * 1024 * 1024

_F32 = jnp.float32
_BF16 = jnp.bfloat16


def _silu(v):
    return v / (1.0 + jnp.exp(-v))


def _adaln_kernel(c_ref, w_ref, b_ref, o_ref):
    cond = _silu(c_ref[...])
    o_ref[...] = jnp.dot(cond, w_ref[...], preferred_element_type=_F32) + b_ref[...]


def _adaln(c, w_ada, b_ada):
    bsz, d = c.shape
    n = w_ada.shape[1]
    bn = 512
    return pl.pallas_call(
        _adaln_kernel,
        grid=(n // bn,),
        in_specs=[
            pl.BlockSpec((bsz, d), lambda j: (0, 0)),
            pl.BlockSpec((d, bn), lambda j: (0, j)),
            pl.BlockSpec((1, bn), lambda j: (0, j)),
        ],
        out_specs=pl.BlockSpec((bsz, bn), lambda j: (0, j)),
        out_shape=jax.ShapeDtypeStruct((bsz, n), _F32),
        name="adaln_mod",
    )(c, w_ada, b_ada.reshape(1, n))


def _inproj_kernel(x_ref, mod_ref, g_ref, w_ref, o_ref):
    x = x_ref[...]
    r = lax.rsqrt(jnp.mean(x * x, axis=-1, keepdims=True) + RMS_EPS)
    shift = mod_ref[:, 0:D_MODEL]
    scale = mod_ref[:, D_MODEL:2 * D_MODEL]
    h = ((x * r) * g_ref[...]) * (1.0 + scale) + shift
    hb = h.astype(_BF16)
    for n0 in range(0, IN_WIDTH, PROJ_COLS):
        o_ref[:, n0:n0 + PROJ_COLS] = jnp.dot(
            hb, w_ref[:, n0:n0 + PROJ_COLS], preferred_element_type=_F32).astype(_BF16)


def _inproj(x, mod3, norm_g, w_in_bf16):
    bsz, s, d = x.shape
    return pl.pallas_call(
        _inproj_kernel,
        grid=(bsz, s // ROW_BLOCK),
        in_specs=[
            pl.BlockSpec((None, ROW_BLOCK, d), lambda b, i: (b, i, 0)),
            pl.BlockSpec((None, 1, 3 * d), lambda b, i: (b, 0, 0)),
            pl.BlockSpec((1, d), lambda b, i: (0, 0)),
            pl.BlockSpec((d, IN_WIDTH), lambda b, i: (0, 0)),
        ],
        out_specs=pl.BlockSpec((None, ROW_BLOCK, IN_WIDTH), lambda b, i: (b, i, 0)),
        out_shape=jax.ShapeDtypeStruct((bsz, s, IN_WIDTH), _BF16),
        compiler_params=pltpu.CompilerParams(
            dimension_semantics=("parallel", "parallel"), vmem_limit_bytes=VMEM_LIMIT),
        name="in_proj",
    )(x, mod3, norm_g.reshape(1, d), w_in_bf16)


def _sb_kernel(q_ref, k_ref, v_ref, g_ref, o_ref, u_ref, q2_ref, acc_ref, carry_ref):
    t = SB_TILE
    seq = q_ref.shape[0]
    row = lax.broadcasted_iota(jnp.int32, (t, t), 0)
    col = lax.broadcasted_iota(jnp.int32, (t, t), 1)
    u_ref[...] = jnp.where(row > col, 1.0, 0.0).astype(_BF16)
    row2 = lax.broadcasted_iota(jnp.int32, (2 * t, t), 0) & (t - 1)
    col2 = lax.broadcasted_iota(jnp.int32, (2 * t, t), 1)
    causal = col2 < row2
    head_a = lax.broadcasted_iota(jnp.int32, (t, LANES), 1) < HEAD_DIM

    def softplus_parts(z):
        zpos = jnp.maximum(z, 0.0)
        zneg = z - zpos
        lse = jnp.log(1.0 + jnp.exp(zneg - zpos))
        return zpos + lse, zneg - lse

    def scores(u, ks, width):
        k = k_ref[pl.ds(ks, width), :]
        return lax.dot_general(q2_ref[u], k, (((1,), (1,)), ((), ())), preferred_element_type=_F32)

    def later_sum(p):
        return jnp.dot(p.astype(_BF16), u_ref[...], preferred_element_type=_F32)

    def diagonal_only(u, qs):
        p, logb = softplus_parts(scores(u, qs, t))
        p = jnp.where(causal, p, 0.0)
        w = jnp.where(causal, jnp.exp(logb - later_sum(p)), 0.0)
        o = jnp.dot(w.astype(_BF16), v_ref[pl.ds(qs, t), :], preferred_element_type=_F32)
        return o, -jnp.sum(p, axis=1, keepdims=True)

    def window(u, qs):
        ws = qs - t if isinstance(qs, int) else pl.multiple_of(qs - t, t)
        p, logb = softplus_parts(scores(u, ws, 2 * t))
        p_prev, p_diag = p[:, 0:t], jnp.where(causal, p[:, t:2 * t], 0.0)
        rs_diag = jnp.sum(p_diag, axis=1, keepdims=True)
        w_diag = jnp.where(causal, jnp.exp(logb[:, t:2 * t] - later_sum(p_diag)), 0.0)
        w_prev = jnp.exp((logb[:, 0:t] - rs_diag) - later_sum(p_prev))
        w = jnp.concatenate([w_prev.astype(_BF16), w_diag.astype(_BF16)], axis=1)
        o = jnp.dot(w, v_ref[pl.ds(ws, 2 * t), :], preferred_element_type=_F32)
        return o, -(rs_diag + jnp.sum(p_prev, axis=1, keepdims=True))

    def tile(u, kj):
        ks = pl.multiple_of(kj * t, t)
        p, logb = softplus_parts(scores(u, ks, t))
        w = jnp.exp(logb - later_sum(p))
        o = jnp.dot(w.astype(_BF16), v_ref[pl.ds(ks, t), :], preferred_element_type=_F32)
        return o, jnp.sum(p, axis=1, keepdims=True)

    def load_q(u, qs):
        q = q_ref[pl.ds(qs, t), :] * jnp.asarray(HEAD_DIM ** -0.5, _BF16)
        zero = jnp.zeros_like(q)
        q2_ref[u] = jnp.concatenate([jnp.where(head_a, q, zero), jnp.where(head_a, zero, q)], axis=0)

    def finish(u, qs):
        acc = acc_ref[u]
        y = jnp.where(head_a, acc[0:t], acc[t:2 * t])
        g = g_ref[pl.ds(qs, t), :].astype(_F32)
        o_ref[pl.ds(qs, t), :] = (y * _silu(g)).astype(_BF16)

    def q_group(first_block, has_prev):
        starts, carry_max = [], []
        for u in range(SB_GROUP):
            qi = first_block + u
            qs = qi * t if isinstance(qi, int) else pl.multiple_of(qi * t, t)
            starts.append(qs)
            load_q(u, qs)
            o, carry = window(u, qs) if (has_prev or u > 0) else diagonal_only(u, qs)
            acc_ref[u] = o
            carry_ref[u] = carry
            carry_max.append(jnp.max(carry))
        for u in range(SB_GROUP):
            qi = first_block + u

            def more(state):
                kj, cmax = state
                return jnp.logical_and(kj >= 0, cmax > -SB_SKIP)

            def k_block(state, u=u):
                kj, _ = state
                o, rs = tile(u, kj)
                c = carry_ref[u]
                acc_ref[u] += jnp.exp(c) * o
                c = c - rs
                carry_ref[u] = c
                return kj - 1, jnp.max(c)

            lax.while_loop(more, k_block, (qi - 2, carry_max[u]))
        for u in range(SB_GROUP):
            finish(u, starts[u])

    q_group(0, False)

    def body(i, _):
        q_group(i * SB_GROUP, True)
        return 0

    lax.fori_loop(1, seq // (t * SB_GROUP), body, 0)


def _sb_attention(proj):
    bsz, s, _ = proj.shape
    t = SB_TILE

    def spec(blk):
        return pl.BlockSpec((None, s, LANES), lambda b, j, blk=blk: (b, 0, blk + j))

    return pl.pallas_call(
        _sb_kernel,
        grid=(bsz, PAIRS),
        in_specs=[spec(SB_Q_BLK), spec(SB_K_BLK), spec(SB_V_BLK), spec(SB_G_BLK)],
        out_specs=pl.BlockSpec((None, s, LANES), lambda b, j: (b, 0, j)),
        out_shape=jax.ShapeDtypeStruct((bsz, s, SB_WIDTH), _BF16),
        scratch_shapes=[
            pltpu.VMEM((t, t), _BF16),
            pltpu.VMEM((SB_GROUP, 2 * t, LANES), _BF16),
            pltpu.VMEM((SB_GROUP, 2 * t, LANES), _F32),
            pltpu.VMEM((SB_GROUP, 2 * t, 1), _F32),
        ],
        compiler_params=pltpu.CompilerParams(
            dimension_semantics=("parallel", "parallel"), vmem_limit_bytes=VMEM_LIMIT),
        name="sb_attention",
    )(proj, proj, proj, proj)


def _sw_kernel(sinks_ref, slopes_ref, q_ref, k_ref, v_ref, g_ref, o_ref, bias_ref, ksel_ref, vext_ref):
    w = WINDOW
    seq = q_ref.shape[0]
    j = pl.program_id(1)
    kvh = j // (PAIRS // SW_KV_HEADS)
    lane = lax.broadcasted_iota(jnp.int32, (w, LANES), 1)
    head_a = lane < HEAD_DIM

    chunk = SW_PREP_ROWS
    low = lax.broadcasted_iota(jnp.int32, (chunk, LANES), 1) < HEAD_DIM

    @pl.when(j == 0)
    def _():
        def prep(i, _):
            rs = pl.multiple_of(i * chunk, chunk)
            kc = k_ref[pl.ds(rs, chunk), :]
            vc = v_ref[pl.ds(rs, chunk), :]
            kr = pltpu.roll(kc, HEAD_DIM, 1)
            vr = pltpu.roll(vc, HEAD_DIM, 1)
            ones = jnp.ones((chunk, LANES), _BF16)
            ksel_ref[0, pl.ds(rs, chunk), :] = jnp.where(low, kc, kr)
            ksel_ref[1, pl.ds(rs, chunk), :] = jnp.where(low, kr, kc)
            vext_ref[0, pl.ds(rs, chunk), 0:LANES] = jnp.where(low, vc, vr)
            vext_ref[1, pl.ds(rs, chunk), 0:LANES] = jnp.where(low, vr, vc)
            vext_ref[0, pl.ds(rs, chunk), LANES:2 * LANES] = ones
            vext_ref[1, pl.ds(rs, chunk), LANES:2 * LANES] = ones
            return 0

        lax.fori_loop(0, seq // chunk, prep, 0)

    row = lax.broadcasted_iota(jnp.int32, (2 * w, 2 * w), 0)
    col = lax.broadcasted_iota(jnp.int32, (2 * w, 2 * w), 1)
    rel = w + (row & (w - 1)) - col
    slope = jnp.where(row < w, slopes_ref[2 * j], slopes_ref[2 * j + 1])
    valid = (rel >= 0) & (rel < w)
    bias_ref[...] = jnp.where(valid, -slope * rel.astype(_F32), NEG_INF)
    rowc = lax.broadcasted_iota(jnp.int32, (2 * w, 1), 0)
    sink = jnp.where(rowc < w, sinks_ref[2 * j], sinks_ref[2 * j + 1])

    def q_block(n, first):
        qs = n * w if isinstance(n, int) else pl.multiple_of(n * w, w)
        q = q_ref[pl.ds(qs, w), :] * jnp.asarray(HEAD_DIM ** -0.5, _BF16)
        zero = jnp.zeros_like(q)
        q2 = jnp.concatenate([jnp.where(head_a, q, zero), jnp.where(head_a, zero, q)], axis=0)
        if first:
            k = ksel_ref[kvh, 0:w, :]
            v = vext_ref[kvh, 0:w, :]
            bias = bias_ref[:, w:2 * w]
        else:
            ks = qs - w if isinstance(n, int) else pl.multiple_of(qs - w, w)
            k = ksel_ref[kvh, pl.ds(ks, 2 * w), :]
            v = vext_ref[kvh, pl.ds(ks, 2 * w), :]
            bias = bias_ref[...]
        logits = lax.dot_general(q2, k, (((1,), (1,)), ((), ())), preferred_element_type=_F32) + bias
        mx = jnp.max(logits, axis=1, keepdims=True)
        p = jnp.exp(logits - mx)
        oe = jnp.dot(p.astype(_BF16), v, preferred_element_type=_F32)
        den2 = oe[:, LANES:2 * LANES] + jnp.exp(sink - mx)
        y = jnp.where(head_a, oe[0:w, 0:LANES], oe[w:2 * w, 0:LANES]) / jnp.where(head_a, den2[0:w], den2[w:2 * w])
        g = g_ref[pl.ds(qs, w), :].astype(_F32)
        o_ref[pl.ds(qs, w), :] = (y * _silu(g)).astype(_BF16)

    for n in range(SW_GROUP):
        q_block(n, n == 0)

    def body(i, _):
        for u in range(SW_GROUP):
            q_block(i * SW_GROUP + u, False)
        return 0

    lax.fori_loop(1, seq // (w * SW_GROUP), body, 0)


def _sw_attention(proj, sinks, slopes):
    bsz, s, _ = proj.shape

    def spec(blk, per_pair):
        if per_pair:
            return pl.BlockSpec((None, s, LANES), lambda b, j, blk=blk: (b, 0, blk + j))
        return pl.BlockSpec((None, s, LANES), lambda b, j, blk=blk: (b, 0, blk))

    smem = pl.BlockSpec(memory_space=pltpu.SMEM)
    return pl.pallas_call(
        _sw_kernel,
        grid=(bsz, PAIRS),
        in_specs=[smem, smem, spec(SW_Q_BLK, True), spec(SW_K_BLK, False), spec(SW_V_BLK, False),
                  spec(SW_G_BLK, True)],
        out_specs=pl.BlockSpec((None, s, LANES), lambda b, j: (b, 0, j)),
        out_shape=jax.ShapeDtypeStruct((bsz, s, SW_WIDTH), _BF16),
        scratch_shapes=[
            pltpu.VMEM((2 * WINDOW, 2 * WINDOW), _F32),
            pltpu.VMEM((SW_KV_HEADS, s, LANES), _BF16),
            pltpu.VMEM((SW_KV_HEADS, s, 2 * LANES), _BF16),
        ],
        compiler_params=pltpu.CompilerParams(
            dimension_semantics=("parallel", "arbitrary"), vmem_limit_bytes=VMEM_LIMIT),
        name="sw_attention",
    )(sinks, slopes, proj, proj, proj, proj)


def _outproj_kernel(x_ref, ysb_ref, ysw_ref, mod_ref, w_ref, fg_ref, o_ref):
    gate = mod_ref[:, 2 * D_MODEL:3 * D_MODEL]
    for r0 in range(0, OUT_ROW_BLOCK, OUT_ROW_CHUNK):
        rows = slice(r0, r0 + OUT_ROW_CHUNK)
        y = jnp.dot(ysb_ref[rows, :], w_ref[0:SB_WIDTH, :], preferred_element_type=_F32)
        y = y + jnp.dot(ysw_ref[rows, :], w_ref[SB_WIDTH:SB_WIDTH + SW_WIDTH, :], preferred_element_type=_F32)
        xn = x_ref[rows, :] + gate * y
        r = lax.rsqrt(jnp.mean(xn * xn, axis=-1, keepdims=True) + RMS_EPS)
        o_ref[rows, :] = (xn * r) * fg_ref[...]


def _outproj(x, y_sb, y_sw, mod3, w_out_bf16, final_g):
    bsz, s, d = x.shape
    return pl.pallas_call(
        _outproj_kernel,
        grid=(bsz, s // OUT_ROW_BLOCK),
        in_specs=[
            pl.BlockSpec((None, OUT_ROW_BLOCK, d), lambda b, i: (b, i, 0)),
            pl.BlockSpec((None, OUT_ROW_BLOCK, SB_WIDTH), lambda b, i: (b, i, 0)),
            pl.BlockSpec((None, OUT_ROW_BLOCK, SW_WIDTH), lambda b, i: (b, i, 0)),
            pl.BlockSpec((None, 1, 3 * d), lambda b, i: (b, 0, 0)),
            pl.BlockSpec((SB_WIDTH + SW_WIDTH, d), lambda b, i: (0, 0)),
            pl.BlockSpec((1, d), lambda b, i: (0, 0)),
        ],
        out_specs=pl.BlockSpec((None, OUT_ROW_BLOCK, d), lambda b, i: (b, i, 0)),
        out_shape=jax.ShapeDtypeStruct((bsz, s, d), _F32),
        compiler_params=pltpu.CompilerParams(
            dimension_semantics=("parallel", "parallel"), vmem_limit_bytes=VMEM_LIMIT),
        name="out_proj",
    )(x, y_sb, y_sw, mod3, w_out_bf16, final_g.reshape(1, d))


def kernel(x, c, w_ada, b_ada, norm_g, w_in, sinks, w_out, final_g):
    assert w_ada.shape[0] == 1, "the output projection kernel fuses the final norm: single layer only"
    bsz = x.shape[0]
    slopes = jnp.asarray([2.0 ** (-8.0 * (h + 1) / SW_HEADS) for h in range(SW_HEADS)], dtype=_F32)
    mod3 = _adaln(c, w_ada[0], b_ada[0]).reshape(bsz, 1, 3 * D_MODEL)
    proj = _inproj(x, mod3, norm_g[0], w_in[0].astype(_BF16))
    y_sb = _sb_attention(proj)
    y_sw = _sw_attention(proj, sinks[0], slopes)
    return _outproj(x, y_sb, y_sw, mod3, w_out[0].astype(_BF16), final_g)
```

```python
import jax
import jax.numpy as jnp
from jax import lax
from jax.experimental import pallas as pl
from jax.experimental.pallas import tpu as pltpu

D_MODEL = 1024
HEAD_DIM = 64
SB_HEADS = 8
SW_HEADS = 8
SW_KV_HEADS = 2
SB_WIDTH = SB_HEADS * HEAD_DIM
SW_WIDTH = SW_HEADS * HEAD_DIM
SW_KV_WIDTH = SW_KV_HEADS * HEAD_DIM
IN_WIDTH = 4 * SB_WIDTH + 2 * SW_WIDTH + 2 * SW_KV_WIDTH
WINDOW = 128
RMS_EPS = 1e-6
NEG_INF = -1e30

LANES = 128
PAIRS = SB_WIDTH // LANES
SB_Q_BLK, SB_K_BLK, SB_V_BLK, SB_G_BLK = 0, PAIRS, 2 * PAIRS, 3 * PAIRS
SW_Q_BLK = 4 * PAIRS
SW_K_BLK = SW_Q_BLK + PAIRS
SW_V_BLK = SW_K_BLK + 1
SW_G_BLK = SW_V_BLK + 1

ROW_BLOCK = 1024
OUT_ROW_BLOCK = 1024
OUT_ROW_CHUNK = 256
FUSE_ROWS = 1024
PROJ_COLS = 256
SB_TILE = 256
SB_GROUP = 4
SW_PREP_ROWS = 512
SW_GROUP = 8
SB_SKIP = 105.0
VMEM_LIMIT = 48 * 1024 * 1024

_F32 = jnp.float32
_BF16 = jnp.bfloat16


def _silu(v):
    return v / (1.0 + jnp.exp(-v))


def _adaln_kernel(c_ref, w_ref, b_ref, o_ref):
    cond = _silu(c_ref[...])
    o_ref[...] = jnp.dot(cond, w_ref[...], preferred_element_type=_F32) + b_ref[...]


def _adaln(c, w_ada, b_ada):
    bsz, d = c.shape
    n = w_ada.shape[1]
    bn = 512
    return pl.pallas_call(
        _adaln_kernel,
        grid=(n // bn,),
        in_specs=[
            pl.BlockSpec((bsz, d), lambda j: (0, 0)),
            pl.BlockSpec((d, bn), lambda j: (0, j)),
            pl.BlockSpec((1, bn), lambda j: (0, j)),
        ],
        out_specs=pl.BlockSpec((bsz, bn), lambda j: (0, j)),
        out_shape=jax.ShapeDtypeStruct((bsz, n), _F32),
        name="adaln_mod",
    )(c, w_ada, b_ada.reshape(1, n))


def _inproj_kernel(x_ref, mod_ref, g_ref, w_ref, o_ref):
    x = x_ref[...]
    r = lax.rsqrt(jnp.mean(x * x, axis=-1, keepdims=True) + RMS_EPS)
    shift = mod_ref[:, 0:D_MODEL]
    scale = mod_ref[:, D_MODEL:2 * D_MODEL]
    h = ((x * r) * g_ref[...]) * (1.0 + scale) + shift
    hb = h.astype(_BF16)
    for n0 in range(0, IN_WIDTH, PROJ_COLS):
        o_ref[:, n0:n0 + PROJ_COLS] = jnp.dot(
            hb, w_ref[:, n0:n0 + PROJ_COLS], preferred_element_type=_F32).astype(_BF16)


def _inproj(x, mod3, norm_g, w_in_bf16):
    bsz, s, d = x.shape
    return pl.pallas_call(
        _inproj_kernel,
        grid=(bsz, s // ROW_BLOCK),
        in_specs=[
            pl.BlockSpec((None, ROW_BLOCK, d), lambda b, i: (b, i, 0)),
            pl.BlockSpec((None, 1, 3 * d), lambda b, i: (b, 0, 0)),
            pl.BlockSpec((1, d), lambda b, i: (0, 0)),
            pl.BlockSpec((d, IN_WIDTH), lambda b, i: (0, 0)),
        ],
        out_specs=pl.BlockSpec((None, ROW_BLOCK, IN_WIDTH), lambda b, i: (b, i, 0)),
        out_shape=jax.ShapeDtypeStruct((bsz, s, IN_WIDTH), _BF16),
        compiler_params=pltpu.CompilerParams(
            dimension_semantics=("parallel", "parallel"), vmem_limit_bytes=VMEM_LIMIT),
        name="in_proj",
    )(x, mod3, norm_g.reshape(1, d), w_in_bf16)


def _sb_kernel(q_ref, k_ref, v_ref, g_ref, o_ref, u_ref, q2_ref, acc_ref, carry_ref):
    t = SB_TILE
    seq = q_ref.shape[0]
    row = lax.broadcasted_iota(jnp.int32, (t, t), 0)
    col = lax.broadcasted_iota(jnp.int32, (t, t), 1)
    u_ref[...] = jnp.where(row > col, 1.0, 0.0).astype(_BF16)
    row2 = lax.broadcasted_iota(jnp.int32, (2 * t, t), 0) & (t - 1)
    col2 = lax.broadcasted_iota(jnp.int32, (2 * t, t), 1)
    causal = col2 < row2
    head_a = lax.broadcasted_iota(jnp.int32, (t, LANES), 1) < HEAD_DIM

    def softplus_parts(z):
        zpos = jnp.maximum(z, 0.0)
        zneg = z - zpos
        lse = jnp.log(1.0 + jnp.exp(zneg - zpos))
        return zpos + lse, zneg - lse

    def scores(u, ks, width):
        k = k_ref[pl.ds(ks, width), :]
        return lax.dot_general(q2_ref[u], k, (((1,), (1,)), ((), ())), preferred_element_type=_F32)

    def later_sum(p):
        return jnp.dot(p.astype(_BF16), u_ref[...], preferred_element_type=_F32)

    def diagonal_only(u, qs):
        p, logb = softplus_parts(scores(u, qs, t))
        p = jnp.where(causal, p, 0.0)
        w = jnp.where(causal, jnp.exp(logb - later_sum(p)), 0.0)
        o = jnp.dot(w.astype(_BF16), v_ref[pl.ds(qs, t), :], preferred_element_type=_F32)
        return o, -jnp.sum(p, axis=1, keepdims=True)

    def window(u, qs):
        ws = qs - t if isinstance(qs, int) else pl.multiple_of(qs - t, t)
        p, logb = softplus_parts(scores(u, ws, 2 * t))
        p_prev, p_diag = p[:, 0:t], jnp.where(causal, p[:, t:2 * t], 0.0)
        rs_diag = jnp.sum(p_diag, axis=1, keepdims=True)
        w_diag = jnp.where(causal, jnp.exp(logb[:, t:2 * t] - later_sum(p_diag)), 0.0)
        w_prev = jnp.exp((logb[:, 0:t] - rs_diag) - later_sum(p_prev))
        w = jnp.concatenate([w_prev.astype(_BF16), w_diag.astype(_BF16)], axis=1)
        o = jnp.dot(w, v_ref[pl.ds(ws, 2 * t), :], preferred_element_type=_F32)
        return o, -(rs_diag + jnp.sum(p_prev, axis=1, keepdims=True))

    def tile(u, kj):
        ks = pl.multiple_of(kj * t, t)
        p, logb = softplus_parts(scores(u, ks, t))
        w = jnp.exp(logb - later_sum(p))
        o = jnp.dot(w.astype(_BF16), v_ref[pl.ds(ks, t), :], preferred_element_type=_F32)
        return o, jnp.sum(p, axis=1, keepdims=True)

    def load_q(u, qs):
        q = q_ref[pl.ds(qs, t), :] * jnp.asarray(HEAD_DIM ** -0.5, _BF16)
        zero = jnp.zeros_like(q)
        q2_ref[u] = jnp.concatenate([jnp.where(head_a, q, zero), jnp.where(head_a, zero, q)], axis=0)

    def finish(u, qs):
        acc = acc_ref[u]
        y = jnp.where(head_a, acc[0:t], acc[t:2 * t])
        g = g_ref[pl.ds(qs, t), :].astype(_F32)
        o_ref[pl.ds(qs, t), :] = (y * _silu(g)).astype(_BF16)

    def q_group(first_block, has_prev):
        starts, carry_max = [], []
        for u in range(SB_GROUP):
            qi = first_block + u
            qs = qi * t if isinstance(qi, int) else pl.multiple_of(qi * t, t)
            starts.append(qs)
            load_q(u, qs)
            o, carry = window(u, qs) if (has_prev or u > 0) else diagonal_only(u, qs)
            acc_ref[u] = o
            carry_ref[u] = carry
            carry_max.append(jnp.max(carry))
        for u in range(SB_GROUP):
            qi = first_block + u

            def more(state):
                kj, cmax = state
                return jnp.logical_and(kj >= 0, cmax > -SB_SKIP)

            def k_block(state, u=u):
                kj, _ = state
                o, rs = tile(u, kj)
                c = carry_ref[u]
                acc_ref[u] += jnp.exp(c) * o
                c = c - rs
                carry_ref[u] = c
                return kj - 1, jnp.max(c)

            lax.while_loop(more, k_block, (qi - 2, carry_max[u]))
        for u in range(SB_GROUP):
            finish(u, starts[u])

    q_group(0, False)

    def body(i, _):
        q_group(i * SB_GROUP, True)
        return 0

    lax.fori_loop(1, seq // (t * SB_GROUP), body, 0)


def _sb_attention(proj):
    bsz, s, _ = proj.shape
    t = SB_TILE

    def spec(blk):
        return pl.BlockSpec((None, s, LANES), lambda b, j, blk=blk: (b, 0, blk + j))

    return pl.pallas_call(
        _sb_kernel,
        grid=(bsz, PAIRS),
        in_specs=[spec(SB_Q_BLK), spec(SB_K_BLK), spec(SB_V_BLK), spec(SB_G_BLK)],
        out_specs=pl.BlockSpec((None, s, LANES), lambda b, j: (b, 0, j)),
        out_shape=jax.ShapeDtypeStruct((bsz, s, SB_WIDTH), _BF16),
        scratch_shapes=[
            pltpu.VMEM((t, t), _BF16),
            pltpu.VMEM((SB_GROUP, 2 * t, LANES), _BF16),
            pltpu.VMEM((SB_GROUP, 2 * t, LANES), _F32),
            pltpu.VMEM((SB_GROUP, 2 * t, 1), _F32),
        ],
        compiler_params=pltpu.CompilerParams(
            dimension_semantics=("parallel", "parallel"), vmem_limit_bytes=VMEM_LIMIT),
        name="sb_attention",
    )(proj, proj, proj, proj)


def _sw_kernel(sinks_ref, slopes_ref, q_ref, k_ref, v_ref, g_ref, o_ref, bias_ref, ksel_ref, vext_ref):
    w = WINDOW
    seq = q_ref.shape[0]
    j = pl.program_id(1)
    kvh = j // (PAIRS // SW_KV_HEADS)
    lane = lax.broadcasted_iota(jnp.int32, (w, LANES), 1)
    head_a = lane < HEAD_DIM

    chunk = SW_PREP_ROWS
    low = lax.broadcasted_iota(jnp.int32, (chunk, LANES), 1) < HEAD_DIM

    @pl.when(j == 0)
    def _():
        def prep(i, _):
            rs = pl.multiple_of(i * chunk, chunk)
            kc = k_ref[pl.ds(rs, chunk), :]
            vc = v_ref[pl.ds(rs, chunk), :]
            kr = pltpu.roll(kc, HEAD_DIM, 1)
            vr = pltpu.roll(vc, HEAD_DIM, 1)
            ones = jnp.ones((chunk, LANES), _BF16)
            ksel_ref[0, pl.ds(rs, chunk), :] = jnp.where(low, kc, kr)
            ksel_ref[1, pl.ds(rs, chunk), :] = jnp.where(low, kr, kc)
            vext_ref[0, pl.ds(rs, chunk), 0:LANES] = jnp.where(low, vc, vr)
            vext_ref[1, pl.ds(rs, chunk), 0:LANES] = jnp.where(low, vr, vc)
            vext_ref[0, pl.ds(rs, chunk), LANES:2 * LANES] = ones
            vext_ref[1, pl.ds(rs, chunk), LANES:2 * LANES] = ones
            return 0

        lax.fori_loop(0, seq // chunk, prep, 0)

    row = lax.broadcasted_iota(jnp.int32, (2 * w, 2 * w), 0)
    col = lax.broadcasted_iota(jnp.int32, (2 * w, 2 * w), 1)
    rel = w + (row & (w - 1)) - col
    slope = jnp.where(row < w, slopes_ref[2 * j], slopes_ref[2 * j + 1])
    valid = (rel >= 0) & (rel < w)
    bias_ref[...] = jnp.where(valid, -slope * rel.astype(_F32), NEG_INF)
    rowc = lax.broadcasted_iota(jnp.int32, (2 * w, 1), 0)
    sink = jnp.where(rowc < w, sinks_ref[2 * j], sinks_ref[2 * j + 1])

    def q_block(n, first):
        qs = n * w if isinstance(n, int) else pl.multiple_of(n * w, w)
        q = q_ref[pl.ds(qs, w), :] * jnp.asarray(HEAD_DIM ** -0.5, _BF16)
        zero = jnp.zeros_like(q)
        q2 = jnp.concatenate([jnp.where(head_a, q, zero), jnp.where(head_a, zero, q)], axis=0)
        if first:
            k = ksel_ref[kvh, 0:w, :]
            v = vext_ref[kvh, 0:w, :]
            bias = bias_ref[:, w:2 * w]
        else:
            ks = qs - w if isinstance(n, int) else pl.multiple_of(qs - w, w)
            k = ksel_ref[kvh, pl.ds(ks, 2 * w), :]
            v = vext_ref[kvh, pl.ds(ks, 2 * w), :]
            bias = bias_ref[...]
        logits = lax.dot_general(q2, k, (((1,), (1,)), ((), ())), preferred_element_type=_F32) + bias
        mx = jnp.max(logits, axis=1, keepdims=True)
        p = jnp.exp(logits - mx)
        oe = jnp.dot(p.astype(_BF16), v, preferred_element_type=_F32)
        den2 = oe[:, LANES:2 * LANES] + jnp.exp(sink - mx)
        y = jnp.where(head_a, oe[0:w, 0:LANES], oe[w:2 * w, 0:LANES]) / jnp.where(head_a, den2[0:w], den2[w:2 * w])
        g = g_ref[pl.ds(qs, w), :].astype(_F32)
        o_ref[pl.ds(qs, w), :] = (y * _silu(g)).astype(_BF16)

    for n in range(SW_GROUP):
        q_block(n, n == 0)

    def body(i, _):
        for u in range(SW_GROUP):
            q_block(i * SW_GROUP + u, False)
        return 0

    lax.fori_loop(1, seq // (w * SW_GROUP), body, 0)


def _sw_attention(proj, sinks, slopes):
    bsz, s, _ = proj.shape

    def spec(blk, per_pair):
        if per_pair:
            return pl.BlockSpec((None, s, LANES), lambda b, j, blk=blk: (b, 0, blk + j))
        return pl.BlockSpec((None, s, LANES), lambda b, j, blk=blk: (b, 0, blk))

    smem = pl.BlockSpec(memory_space=pltpu.SMEM)
    return pl.pallas_call(
        _sw_kernel,
        grid=(bsz, PAIRS),
        in_specs=[smem, smem, spec(SW_Q_BLK, True), spec(SW_K_BLK, False), spec(SW_V_BLK, False),
                  spec(SW_G_BLK, True)],
        out_specs=pl.BlockSpec((None, s, LANES), lambda b, j: (b, 0, j)),
        out_shape=jax.ShapeDtypeStruct((bsz, s, SW_WIDTH), _BF16),
        scratch_shapes=[
            pltpu.VMEM((2 * WINDOW, 2 * WINDOW), _F32),
            pltpu.VMEM((SW_KV_HEADS, s, LANES), _BF16),
            pltpu.VMEM((SW_KV_HEADS, s, 2 * LANES), _BF16),
        ],
        compiler_params=pltpu.CompilerParams(
            dimension_semantics=("parallel", "arbitrary"), vmem_limit_bytes=VMEM_LIMIT),
        name="sw_attention",
    )(sinks, slopes, proj, proj, proj, proj)


def _outproj_kernel(x_ref, ysb_ref, ysw_ref, mod_ref, w_ref, fg_ref, o_ref):
    gate = mod_ref[:, 2 * D_MODEL:3 * D_MODEL]
    for r0 in range(0, OUT_ROW_BLOCK, OUT_ROW_CHUNK):
        rows = slice(r0, r0 + OUT_ROW_CHUNK)
        y = jnp.dot(ysb_ref[rows, :], w_ref[0:SB_WIDTH, :], preferred_element_type=_F32)
        y = y + jnp.dot(ysw_ref[rows, :], w_ref[SB_WIDTH:SB_WIDTH + SW_WIDTH, :], preferred_element_type=_F32)
        xn = x_ref[rows, :] + gate * y
        r = lax.rsqrt(jnp.mean(xn * xn, axis=-1, keepdims=True) + RMS_EPS)
        o_ref[rows, :] = (xn * r) * fg_ref[...]


def _outproj(x, y_sb, y_sw, mod3, w_out_bf16, final_g):
    bsz, s, d = x.shape
    return pl.pallas_call(
        _outproj_kernel,
        grid=(bsz, s // OUT_ROW_BLOCK),
        in_specs=[
            pl.BlockSpec((None, OUT_ROW_BLOCK, d), lambda b, i: (b, i, 0)),
            pl.BlockSpec((None, OUT_ROW_BLOCK, SB_WIDTH), lambda b, i: (b, i, 0)),
            pl.BlockSpec((None, OUT_ROW_BLOCK, SW_WIDTH), lambda b, i: (b, i, 0)),
            pl.BlockSpec((None, 1, 3 * d), lambda b, i: (b, 0, 0)),
            pl.BlockSpec((SB_WIDTH + SW_WIDTH, d), lambda b, i: (0, 0)),
            pl.BlockSpec((1, d), lambda b, i: (0, 0)),
        ],
        out_specs=pl.BlockSpec((None, OUT_ROW_BLOCK, d), lambda b, i: (b, i, 0)),
        out_shape=jax.ShapeDtypeStruct((bsz, s, d), _F32),
        compiler_params=pltpu.CompilerParams(
            dimension_semantics=("parallel", "parallel"), vmem_limit_bytes=VMEM_LIMIT),
        name="out_proj",
    )(x, y_sb, y_sw, mod3, w_out_bf16, final_g.reshape(1, d))


def _swout_kernel(sinks_ref, q_ref, g_lo_ref, g_hi_ref, k_ref, v_ref, x_ref, ysb_ref, mod_ref, w_ref, fg_ref,
                  o_ref, bias_ref, ksel_ref, vext_ref):
    w = WINDOW
    seq = k_ref.shape[0]
    i = pl.program_id(1)
    head_a = lax.broadcasted_iota(jnp.int32, (w, LANES), 1) < HEAD_DIM

    @pl.when(i == 0)
    def _():
        chunk = SW_PREP_ROWS
        low = lax.broadcasted_iota(jnp.int32, (chunk, LANES), 1) < HEAD_DIM
        for kvh in range(SW_KV_HEADS):
            ksel_ref[kvh, 0:w, :] = jnp.zeros((w, LANES), _BF16)
            vext_ref[kvh, 0:w, :] = jnp.zeros((w, 2 * LANES), _BF16)

        def prep(c, _):
            rs = pl.multiple_of(c * chunk, chunk)
            kc = k_ref[pl.ds(rs, chunk), :]
            vc = v_ref[pl.ds(rs, chunk), :]
            kr = pltpu.roll(kc, HEAD_DIM, 1)
            vr = pltpu.roll(vc, HEAD_DIM, 1)
            ones = jnp.ones((chunk, LANES), _BF16)
            dst = pl.ds(rs + w, chunk)
            ksel_ref[0, dst, :] = jnp.where(low, kc, kr)
            ksel_ref[1, dst, :] = jnp.where(low, kr, kc)
            vext_ref[0, dst, 0:LANES] = jnp.where(low, vc, vr)
            vext_ref[1, dst, 0:LANES] = jnp.where(low, vr, vc)
            vext_ref[0, dst, LANES:2 * LANES] = ones
            vext_ref[1, dst, LANES:2 * LANES] = ones
            return 0

        lax.fori_loop(0, seq // chunk, prep, 0)

        row = lax.broadcasted_iota(jnp.int32, (2 * w, 2 * w), 0)
        col = lax.broadcasted_iota(jnp.int32, (2 * w, 2 * w), 1)
        rel = w + (row & (w - 1)) - col
        valid = (rel >= 0) & (rel < w)
        for j in range(PAIRS):
            slope = jnp.where(row < w, _alibi_slope(2 * j), _alibi_slope(2 * j + 1))
            bias = jnp.where(valid, -slope * rel.astype(_F32), NEG_INF)
            bias_ref[j, 0] = bias
            bias_ref[j, 1] = jnp.where(col < w, NEG_INF, bias)

    rowc = lax.broadcasted_iota(jnp.int32, (2 * w, 1), 0)
    gate = mod_ref[:, 2 * D_MODEL:3 * D_MODEL]
    scale = jnp.asarray(HEAD_DIM ** -0.5, _BF16)

    def attend(r0, j, n):
        q = q_ref[r0:r0 + w, j * LANES:(j + 1) * LANES] * scale
        zero = jnp.zeros_like(q)
        q2 = jnp.concatenate([jnp.where(head_a, q, zero), jnp.where(head_a, zero, q)], axis=0)
        kvh = j // (PAIRS // SW_KV_HEADS)
        ks = pl.multiple_of(n * w, w)
        k = ksel_ref[kvh, pl.ds(ks, 2 * w), :]
        v = vext_ref[kvh, pl.ds(ks, 2 * w), :]
        bias = bias_ref[j, jnp.where(n == 0, 1, 0)]
        logits = lax.dot_general(q2, k, (((1,), (1,)), ((), ())), preferred_element_type=_F32) + bias
        mx = jnp.max(logits, axis=1, keepdims=True)
        p = jnp.exp(logits - mx)
        oe = jnp.dot(p.astype(_BF16), v, preferred_element_type=_F32)
        sink = jnp.where(rowc < w, sinks_ref[2 * j], sinks_ref[2 * j + 1])
        den2 = oe[:, LANES:2 * LANES] + jnp.exp(sink - mx)
        y = jnp.where(head_a, oe[0:w, 0:LANES], oe[w:2 * w, 0:LANES]) / jnp.where(head_a, den2[0:w], den2[w:2 * w])
        g_ref_j = g_lo_ref if j < PAIRS // 2 else g_hi_ref
        gl = (j % (PAIRS // 2)) * LANES
        g = g_ref_j[r0:r0 + w, gl:gl + LANES].astype(_F32)
        return (y * _silu(g)).astype(_BF16)

    for r0 in range(0, FUSE_ROWS, OUT_ROW_CHUNK):
        blocks = []
        for b0 in range(r0, r0 + OUT_ROW_CHUNK, w):
            n = i * (FUSE_ROWS // w) + b0 // w
            blocks.append(jnp.concatenate([attend(b0, j, n) for j in range(PAIRS)], axis=1))
        y_sw = jnp.concatenate(blocks, axis=0)
        rows = slice(r0, r0 + OUT_ROW_CHUNK)
        y = jnp.dot(ysb_ref[rows, :], w_ref[0:SB_WIDTH, :], preferred_element_type=_F32)
        y = y + jnp.dot(y_sw, w_ref[SB_WIDTH:SB_WIDTH + SW_WIDTH, :], preferred_element_type=_F32)
        xn = x_ref[rows, :] + gate * y
        r = lax.rsqrt(jnp.mean(xn * xn, axis=-1, keepdims=True) + RMS_EPS)
        o_ref[rows, :] = (xn * r) * fg_ref[...]


def _alibi_slope(head):
    return 2.0 ** (-8.0 * (head + 1) / SW_HEADS)


def _sw_outproj(proj, x, y_sb, mod3, sinks, w_out_bf16, final_g):
    bsz, s, d = x.shape
    rb = FUSE_ROWS
    half = SW_WIDTH // 2
    q_blk = (SW_Q_BLK * LANES) // SW_WIDTH
    g_blk = (SW_G_BLK * LANES) // half
    return pl.pallas_call(
        _swout_kernel,
        grid=(bsz, s // rb),
        in_specs=[
            pl.BlockSpec(memory_space=pltpu.SMEM),
            pl.BlockSpec((None, rb, SW_WIDTH), lambda b, i: (b, i, q_blk)),
            pl.BlockSpec((None, rb, half), lambda b, i: (b, i, g_blk)),
            pl.BlockSpec((None, rb, half), lambda b, i: (b, i, g_blk + 1)),
            pl.BlockSpec((None, s, LANES), lambda b, i: (b, 0, SW_K_BLK)),
            pl.BlockSpec((None, s, LANES), lambda b, i: (b, 0, SW_V_BLK)),
            pl.BlockSpec((None, rb, d), lambda b, i: (b, i, 0)),
            pl.BlockSpec((None, rb, SB_WIDTH), lambda b, i: (b, i, 0)),
            pl.BlockSpec((None, 1, 3 * d), lambda b, i: (b, 0, 0)),
            pl.BlockSpec((SB_WIDTH + SW_WIDTH, d), lambda b, i: (0, 0)),
            pl.BlockSpec((1, d), lambda b, i: (0, 0)),
        ],
        out_specs=pl.BlockSpec((None, rb, d), lambda b, i: (b, i, 0)),
        out_shape=jax.ShapeDtypeStruct((bsz, s, d), _F32),
        scratch_shapes=[
            pltpu.VMEM((PAIRS, 2, 2 * WINDOW, 2 * WINDOW), _F32),
            pltpu.VMEM((SW_KV_HEADS, s + WINDOW, LANES), _BF16),
            pltpu.VMEM((SW_KV_HEADS, s + WINDOW, 2 * LANES), _BF16),
        ],
        compiler_params=pltpu.CompilerParams(
            dimension_semantics=("parallel", "arbitrary"), vmem_limit_bytes=VMEM_LIMIT),
        name="sw_attention_out_proj",
    )(sinks, proj, proj, proj, proj, proj, x, y_sb, mod3, w_out_bf16, final_g.reshape(1, d))


def kernel(x, c, w_ada, b_ada, norm_g, w_in, sinks, w_out, final_g):
    assert w_ada.shape[0] == 1, "the output projection kernel fuses the final norm: single layer only"
    bsz = x.shape[0]
    mod3 = _adaln(c, w_ada[0], b_ada[0]).reshape(bsz, 1, 3 * D_MODEL)
    proj = _inproj(x, mod3, norm_g[0], w_in[0].astype(_BF16))
    y_sb = _sb_attention(proj)
    return _sw_outproj(proj, x, y_sb, mod3, sinks[0], w_out[0].astype(_BF16), final_g)
```

```python
import jax
import jax.numpy as jnp
from jax import lax
from jax.experimental import pallas as pl
from jax.experimental.pallas import tpu as pltpu

D_MODEL = 1024
HEAD_DIM = 64
SB_HEADS = 8
SW_HEADS = 8
SW_KV_HEADS = 2
SB_WIDTH = SB_HEADS * HEAD_DIM
SW_WIDTH = SW_HEADS * HEAD_DIM
SW_KV_WIDTH = SW_KV_HEADS * HEAD_DIM
IN_WIDTH = 4 * SB_WIDTH + 2 * SW_WIDTH + 2 * SW_KV_WIDTH
WINDOW = 128
RMS_EPS = 1e-6
NEG_INF = -1e30

LANES = 128
PAIRS = SB_WIDTH // LANES
SB_Q_BLK, SB_K_BLK, SB_V_BLK, SB_G_BLK = 0, PAIRS, 2 * PAIRS, 3 * PAIRS
SW_Q_BLK = 4 * PAIRS
SW_K_BLK = SW_Q_BLK + PAIRS
SW_V_BLK = SW_K_BLK + 1
SW_G_BLK = SW_V_BLK + 1

ROW_BLOCK = 1024
FUSE_ROWS = 1024
OUT_ROW_CHUNK = 512
PROJ_COLS = 256
SB_TILE = 256
SB_GROUP = 8
SW_PREP_ROWS = 512
SB_SKIP = 105.0
VMEM_LIMIT = 48 * 1024 * 1024

_F32 = jnp.float32
_BF16 = jnp.bfloat16


def _silu(v):
    return v / (1.0 + jnp.exp(-v))


def _adaln_kernel(c_ref, w_ref, b_ref, o_ref):
    cond = _silu(c_ref[...])
    o_ref[...] = jnp.dot(cond, w_ref[...], preferred_element_type=_F32) + b_ref[...]


def _adaln(c, w_ada, b_ada):
    bsz, d = c.shape
    n = w_ada.shape[1]
    bn = 512
    return pl.pallas_call(
        _adaln_kernel,
        grid=(n // bn,),
        in_specs=[
            pl.BlockSpec((bsz, d), lambda j: (0, 0)),
            pl.BlockSpec((d, bn), lambda j: (0, j)),
            pl.BlockSpec((1, bn), lambda j: (0, j)),
        ],
        out_specs=pl.BlockSpec((bsz, bn), lambda j: (0, j)),
        out_shape=jax.ShapeDtypeStruct((bsz, n), _F32),
        name="adaln_mod",
    )(c, w_ada, b_ada.reshape(1, n))


def _inproj_kernel(x_ref, mod_ref, g_ref, w_ref, o_ref):
    x = x_ref[...]
    r = lax.rsqrt(jnp.mean(x * x, axis=-1, keepdims=True) + RMS_EPS)
    shift = mod_ref[:, 0:D_MODEL]
    scale = mod_ref[:, D_MODEL:2 * D_MODEL]
    h = ((x * r) * g_ref[...]) * (1.0 + scale) + shift
    hb = h.astype(_BF16)
    for n0 in range(0, IN_WIDTH, PROJ_COLS):
        o_ref[:, n0:n0 + PROJ_COLS] = jnp.dot(
            hb, w_ref[:, n0:n0 + PROJ_COLS], preferred_element_type=_F32).astype(_BF16)


def _inproj(x, mod3, norm_g, w_in_bf16):
    bsz, s, d = x.shape
    return pl.pallas_call(
        _inproj_kernel,
        grid=(bsz, s // ROW_BLOCK),
        in_specs=[
            pl.BlockSpec((None, ROW_BLOCK, d), lambda b, i: (b, i, 0)),
            pl.BlockSpec((None, 1, 3 * d), lambda b, i: (b, 0, 0)),
            pl.BlockSpec((1, d), lambda b, i: (0, 0)),
            pl.BlockSpec((d, IN_WIDTH), lambda b, i: (0, 0)),
        ],
        out_specs=pl.BlockSpec((None, ROW_BLOCK, IN_WIDTH), lambda b, i: (b, i, 0)),
        out_shape=jax.ShapeDtypeStruct((bsz, s, IN_WIDTH), _BF16),
        compiler_params=pltpu.CompilerParams(
            dimension_semantics=("parallel", "parallel"), vmem_limit_bytes=VMEM_LIMIT),
        name="in_proj",
    )(x, mod3, norm_g.reshape(1, d), w_in_bf16)


def _sb_kernel(q_ref, k_ref, v_ref, g_ref, o_ref, u_ref, q2_ref, acc_ref, carry_ref):
    t = SB_TILE
    seq = q_ref.shape[0]
    row = lax.broadcasted_iota(jnp.int32, (t, t), 0)
    col = lax.broadcasted_iota(jnp.int32, (t, t), 1)
    u_ref[...] = jnp.where(row > col, 1.0, 0.0).astype(_BF16)
    row2 = lax.broadcasted_iota(jnp.int32, (2 * t, t), 0) & (t - 1)
    col2 = lax.broadcasted_iota(jnp.int32, (2 * t, t), 1)
    causal = col2 < row2
    head_a = lax.broadcasted_iota(jnp.int32, (t, LANES), 1) < HEAD_DIM

    def softplus_parts(z):
        zpos = jnp.maximum(z, 0.0)
        zneg = z - zpos
        lse = jnp.log(1.0 + jnp.exp(zneg - zpos))
        return zpos + lse, zneg - lse

    def scores(u, ks, width):
        k = k_ref[pl.ds(ks, width), :]
        return lax.dot_general(q2_ref[u], k, (((1,), (1,)), ((), ())), preferred_element_type=_F32)

    def later_sum(p):
        return jnp.dot(p.astype(_BF16), u_ref[...], preferred_element_type=_F32)

    def diagonal_only(u, qs):
        p, logb = softplus_parts(scores(u, qs, t))
        p = jnp.where(causal, p, 0.0)
        w = jnp.where(causal, jnp.exp(logb - later_sum(p)), 0.0)
        o = jnp.dot(w.astype(_BF16), v_ref[pl.ds(qs, t), :], preferred_element_type=_F32)
        return o, -jnp.sum(p, axis=1, keepdims=True)

    def window(u, qs):
        ws = qs - t if isinstance(qs, int) else pl.multiple_of(qs - t, t)
        p, logb = softplus_parts(scores(u, ws, 2 * t))
        p_prev, p_diag = p[:, 0:t], jnp.where(causal, p[:, t:2 * t], 0.0)
        rs_diag = jnp.sum(p_diag, axis=1, keepdims=True)
        w_diag = jnp.where(causal, jnp.exp(logb[:, t:2 * t] - later_sum(p_diag)), 0.0)
        w_prev = jnp.exp((logb[:, 0:t] - rs_diag) - later_sum(p_prev))
        w = jnp.concatenate([w_prev.astype(_BF16), w_diag.astype(_BF16)], axis=1)
        o = jnp.dot(w, v_ref[pl.ds(ws, 2 * t), :], preferred_element_type=_F32)
        return o, -(rs_diag + jnp.sum(p_prev, axis=1, keepdims=True))

    def tile(u, kj):
        ks = pl.multiple_of(kj * t, t)
        p, logb = softplus_parts(scores(u, ks, t))
        w = jnp.exp(logb - later_sum(p))
        o = jnp.dot(w.astype(_BF16), v_ref[pl.ds(ks, t), :], preferred_element_type=_F32)
        return o, jnp.sum(p, axis=1, keepdims=True)

    def load_q(u, qs):
        q = q_ref[pl.ds(qs, t), :] * jnp.asarray(HEAD_DIM ** -0.5, _BF16)
        zero = jnp.zeros_like(q)
        q2_ref[u] = jnp.concatenate([jnp.where(head_a, q, zero), jnp.where(head_a, zero, q)], axis=0)

    def finish(u, qs):
        acc = acc_ref[u]
        y = jnp.where(head_a, acc[0:t], acc[t:2 * t])
        g = g_ref[pl.ds(qs, t), :].astype(_F32)
        o_ref[pl.ds(qs, t), :] = (y * _silu(g)).astype(_BF16)

    def q_group(first_block, has_prev):
        starts, carry_max = [], []
        for u in range(SB_GROUP):
            qi = first_block + u
            qs = qi * t if isinstance(qi, int) else pl.multiple_of(qi * t, t)
            starts.append(qs)
            load_q(u, qs)
            o, carry = window(u, qs) if (has_prev or u > 0) else diagonal_only(u, qs)
            acc_ref[u] = o
            carry_ref[u] = carry
            carry_max.append(jnp.max(carry))
        for u in range(SB_GROUP):
            qi = first_block + u

            def more(state):
                kj, cmax = state
                return jnp.logical_and(kj >= 0, cmax > -SB_SKIP)

            def k_block(state, u=u):
                kj, _ = state
                o, rs = tile(u, kj)
                c = carry_ref[u]
                acc_ref[u] += jnp.exp(c) * o
                c = c - rs
                carry_ref[u] = c
                return kj - 1, jnp.max(c)

            lax.while_loop(more, k_block, (qi - 2, carry_max[u]))
        for u in range(SB_GROUP):
            finish(u, starts[u])

    q_group(0, False)

    def body(i, _):
        q_group(i * SB_GROUP, True)
        return 0

    lax.fori_loop(1, seq // (t * SB_GROUP), body, 0)


def _sb_attention(proj):
    bsz, s, _ = proj.shape
    t = SB_TILE

    def spec(blk):
        return pl.BlockSpec((None, s, LANES), lambda b, j, blk=blk: (b, 0, blk + j))

    return pl.pallas_call(
        _sb_kernel,
        grid=(bsz, PAIRS),
        in_specs=[spec(SB_Q_BLK), spec(SB_K_BLK), spec(SB_V_BLK), spec(SB_G_BLK)],
        out_specs=pl.BlockSpec((None, s, LANES), lambda b, j: (b, 0, j)),
        out_shape=jax.ShapeDtypeStruct((bsz, s, SB_WIDTH), _BF16),
        scratch_shapes=[
            pltpu.VMEM((t, t), _BF16),
            pltpu.VMEM((SB_GROUP, 2 * t, LANES), _BF16),
            pltpu.VMEM((SB_GROUP, 2 * t, LANES), _F32),
            pltpu.VMEM((SB_GROUP, 2 * t, 1), _F32),
        ],
        compiler_params=pltpu.CompilerParams(
            dimension_semantics=("parallel", "parallel"), vmem_limit_bytes=VMEM_LIMIT),
        name="sb_attention",
    )(proj, proj, proj, proj)


def _swout_kernel(sinks_ref, q_ref, g_lo_ref, g_hi_ref, k_ref, v_ref, x_ref, ysb_ref, mod_ref, w_ref, fg_ref,
                  o_ref, bias_ref, ksel_ref, vext_ref):
    w = WINDOW
    seq = k_ref.shape[0]
    i = pl.program_id(1)
    head_a = lax.broadcasted_iota(jnp.int32, (w, LANES), 1) < HEAD_DIM

    @pl.when(i == 0)
    def _():
        chunk = SW_PREP_ROWS
        low = lax.broadcasted_iota(jnp.int32, (chunk, LANES), 1) < HEAD_DIM
        for kvh in range(SW_KV_HEADS):
            ksel_ref[kvh, 0:w, :] = jnp.zeros((w, LANES), _BF16)
            vext_ref[kvh, 0:w, :] = jnp.zeros((w, 2 * LANES), _BF16)

        def prep(c, _):
            rs = pl.multiple_of(c * chunk, chunk)
            kc = k_ref[pl.ds(rs, chunk), :]
            vc = v_ref[pl.ds(rs, chunk), :]
            kr = pltpu.roll(kc, HEAD_DIM, 1)
            vr = pltpu.roll(vc, HEAD_DIM, 1)
            ones = jnp.ones((chunk, LANES), _BF16)
            dst = pl.ds(rs + w, chunk)
            ksel_ref[0, dst, :] = jnp.where(low, kc, kr)
            ksel_ref[1, dst, :] = jnp.where(low, kr, kc)
            vext_ref[0, dst, 0:LANES] = jnp.where(low, vc, vr)
            vext_ref[1, dst, 0:LANES] = jnp.where(low, vr, vc)
            vext_ref[0, dst, LANES:2 * LANES] = ones
            vext_ref[1, dst, LANES:2 * LANES] = ones
            return 0

        lax.fori_loop(0, seq // chunk, prep, 0)

        row = lax.broadcasted_iota(jnp.int32, (2 * w, 2 * w), 0)
        col = lax.broadcasted_iota(jnp.int32, (2 * w, 2 * w), 1)
        rel = w + (row & (w - 1)) - col
        valid = (rel >= 0) & (rel < w)
        for j in range(PAIRS):
            slope = jnp.where(row < w, _alibi_slope(2 * j), _alibi_slope(2 * j + 1))
            bias = jnp.where(valid, -slope * rel.astype(_F32), NEG_INF)
            bias_ref[j, 0] = bias
            bias_ref[j, 1] = jnp.where(col < w, NEG_INF, bias)

    rowc = lax.broadcasted_iota(jnp.int32, (2 * w, 1), 0)
    gate = mod_ref[:, 2 * D_MODEL:3 * D_MODEL]
    scale = jnp.asarray(HEAD_DIM ** -0.5, _BF16)

    def attend(r0, j, n):
        q = q_ref[r0:r0 + w, j * LANES:(j + 1) * LANES] * scale
        zero = jnp.zeros_like(q)
        q2 = jnp.concatenate([jnp.where(head_a, q, zero), jnp.where(head_a, zero, q)], axis=0)
        kvh = j // (PAIRS // SW_KV_HEADS)
        ks = pl.multiple_of(n * w, w)
        k = ksel_ref[kvh, pl.ds(ks, 2 * w), :]
        v = vext_ref[kvh, pl.ds(ks, 2 * w), :]
        bias = bias_ref[j, jnp.where(n == 0, 1, 0)]
        logits = lax.dot_general(q2, k, (((1,), (1,)), ((), ())), preferred_element_type=_F32) + bias
        mx = jnp.max(logits, axis=1, keepdims=True)
        p = jnp.exp(logits - mx)
        oe = jnp.dot(p.astype(_BF16), v, preferred_element_type=_F32)
        sink = jnp.where(rowc < w, sinks_ref[2 * j], sinks_ref[2 * j + 1])
        den2 = oe[:, LANES:2 * LANES] + jnp.exp(sink - mx)
        y = jnp.where(head_a, oe[0:w, 0:LANES], oe[w:2 * w, 0:LANES]) / jnp.where(head_a, den2[0:w], den2[w:2 * w])
        g_ref_j = g_lo_ref if j < PAIRS // 2 else g_hi_ref
        gl = (j % (PAIRS // 2)) * LANES
        g = g_ref_j[r0:r0 + w, gl:gl + LANES].astype(_F32)
        return (y * _silu(g)).astype(_BF16)

    for r0 in range(0, FUSE_ROWS, OUT_ROW_CHUNK):
        blocks = []
        for b0 in range(r0, r0 + OUT_ROW_CHUNK, w):
            n = i * (FUSE_ROWS // w) + b0 // w
            blocks.append(jnp.concatenate([attend(b0, j, n) for j in range(PAIRS)], axis=1))
        y_sw = jnp.concatenate(blocks, axis=0)
        rows = slice(r0, r0 + OUT_ROW_CHUNK)
        y = jnp.dot(ysb_ref[rows, :], w_ref[0:SB_WIDTH, :], preferred_element_type=_F32)
        y = y + jnp.dot(y_sw, w_ref[SB_WIDTH:SB_WIDTH + SW_WIDTH, :], preferred_element_type=_F32)
        xn = x_ref[rows, :] + gate * y
        r = lax.rsqrt(jnp.mean(xn * xn, axis=-1, keepdims=True) + RMS_EPS)
        o_ref[rows, :] = (xn * r) * fg_ref[...]


def _alibi_slope(head):
    return 2.0 ** (-8.0 * (head + 1) / SW_HEADS)


def _sw_outproj(proj, x, y_sb, mod3, sinks, w_out_bf16, final_g):
    bsz, s, d = x.shape
    rb = FUSE_ROWS
    half = SW_WIDTH // 2
    q_blk = (SW_Q_BLK * LANES) // SW_WIDTH
    g_blk = (SW_G_BLK * LANES) // half
    return pl.pallas_call(
        _swout_kernel,
        grid=(bsz, s // rb),
        in_specs=[
            pl.BlockSpec(memory_space=pltpu.SMEM),
            pl.BlockSpec((None, rb, SW_WIDTH), lambda b, i: (b, i, q_blk)),
            pl.BlockSpec((None, rb, half), lambda b, i: (b, i, g_blk)),
            pl.BlockSpec((None, rb, half), lambda b, i: (b, i, g_blk + 1)),
            pl.BlockSpec((None, s, LANES), lambda b, i: (b, 0, SW_K_BLK)),
            pl.BlockSpec((None, s, LANES), lambda b, i: (b, 0, SW_V_BLK)),
            pl.BlockSpec((None, rb, d), lambda b, i: (b, i, 0)),
            pl.BlockSpec((None, rb, SB_WIDTH), lambda b, i: (b, i, 0)),
            pl.BlockSpec((None, 1, 3 * d), lambda b, i: (b, 0, 0)),
            pl.BlockSpec((SB_WIDTH + SW_WIDTH, d), lambda b, i: (0, 0)),
            pl.BlockSpec((1, d), lambda b, i: (0, 0)),
        ],
        out_specs=pl.BlockSpec((None, rb, d), lambda b, i: (b, i, 0)),
        out_shape=jax.ShapeDtypeStruct((bsz, s, d), _F32),
        scratch_shapes=[
            pltpu.VMEM((PAIRS, 2, 2 * WINDOW, 2 * WINDOW), _F32),
            pltpu.VMEM((SW_KV_HEADS, s + WINDOW, LANES), _BF16),
            pltpu.VMEM((SW_KV_HEADS, s + WINDOW, 2 * LANES), _BF16),
        ],
        compiler_params=pltpu.CompilerParams(
            dimension_semantics=("parallel", "arbitrary"), vmem_limit_bytes=VMEM_LIMIT),
        name="sw_attention_out_proj",
    )(sinks, proj, proj, proj, proj, proj, x, y_sb, mod3, w_out_bf16, final_g.reshape(1, d))


def kernel(x, c, w_ada, b_ada, norm_g, w_in, sinks, w_out, final_g):
    assert w_ada.shape[0] == 1, "the output projection kernel fuses the final norm: single layer only"
    bsz = x.shape[0]
    mod3 = _adaln(c, w_ada[0], b_ada[0]).reshape(bsz, 1, 3 * D_MODEL)
    proj = _inproj(x, mod3, norm_g[0], w_in[0].astype(_BF16))
    y_sb = _sb_attention(proj)
    return _sw_outproj(proj, x, y_sb, mod3, sinks[0], w_out[0].astype(_BF16), final_g)
```

```python
import jax
import jax.numpy as jnp
from jax import lax
from jax.experimental import pallas as pl
from jax.experimental.pallas import tpu as pltpu

D_MODEL = 1024
HEAD_DIM = 64
SB_HEADS = 8
SW_HEADS = 8
SW_KV_HEADS = 2
SB_WIDTH = SB_HEADS * HEAD_DIM
SW_WIDTH = SW_HEADS * HEAD_DIM
SW_KV_WIDTH = SW_KV_HEADS * HEAD_DIM
IN_WIDTH = 4 * SB_WIDTH + 2 * SW_WIDTH + 2 * SW_KV_WIDTH
WINDOW = 128
RMS_EPS = 1e-6
NEG_INF = -1e30

LANES = 128
PAIRS = SB_WIDTH // LANES
SB_Q_BLK, SB_K_BLK, SB_V_BLK, SB_G_BLK = 0, PAIRS, 2 * PAIRS, 3 * PAIRS
SW_Q_BLK = 4 * PAIRS
SW_K_BLK = SW_Q_BLK + PAIRS
SW_V_BLK = SW_K_BLK + 1
SW_G_BLK = SW_V_BLK + 1

ROW_BLOCK = 1024
ROW_CHUNK = 512
FUSE_ROWS = 1024
OUT_ROW_CHUNK = 512
PROJ_COLS = 256
SB_TILE = 256
SB_GROUP = 8
SW_PREP_ROWS = 512
SB_SKIP = 105.0
VMEM_LIMIT = 48 * 1024 * 1024

_F32 = jnp.float32
_BF16 = jnp.bfloat16


def _silu(v):
    return v / (1.0 + jnp.exp(-v))


def _adaln_kernel(c_ref, w_ref, b_ref, o_ref):
    cond = _silu(c_ref[...])
    o_ref[...] = jnp.dot(cond, w_ref[...], preferred_element_type=_F32) + b_ref[...]


def _adaln(c, w_ada, b_ada):
    bsz, d = c.shape
    n = w_ada.shape[1]
    bn = 512
    return pl.pallas_call(
        _adaln_kernel,
        grid=(n // bn,),
        in_specs=[
            pl.BlockSpec((bsz, d), lambda j: (0, 0)),
            pl.BlockSpec((d, bn), lambda j: (0, j)),
            pl.BlockSpec((1, bn), lambda j: (0, j)),
        ],
        out_specs=pl.BlockSpec((bsz, bn), lambda j: (0, j)),
        out_shape=jax.ShapeDtypeStruct((bsz, n), _F32),
        name="adaln_mod",
    )(c, w_ada, b_ada.reshape(1, n))


def _inproj_kernel(x_ref, mod_ref, g_ref, w_ref, o_ref, wb_ref):
    @pl.when((pl.program_id(0) == 0) & (pl.program_id(1) == 0))
    def _():
        for n0 in range(0, IN_WIDTH, PROJ_COLS):
            wb_ref[:, n0:n0 + PROJ_COLS] = w_ref[:, n0:n0 + PROJ_COLS].astype(_BF16)

    shift = mod_ref[:, 0:D_MODEL]
    scale = mod_ref[:, D_MODEL:2 * D_MODEL]
    for r0 in range(0, ROW_BLOCK, ROW_CHUNK):
        x = x_ref[r0:r0 + ROW_CHUNK, :]
        r = lax.rsqrt(jnp.mean(x * x, axis=-1, keepdims=True) + RMS_EPS)
        h = ((x * r) * g_ref[...]) * (1.0 + scale) + shift
        hb = h.astype(_BF16)
        for n0 in range(0, IN_WIDTH, PROJ_COLS):
            o_ref[r0:r0 + ROW_CHUNK, n0:n0 + PROJ_COLS] = jnp.dot(
                hb, wb_ref[:, n0:n0 + PROJ_COLS], preferred_element_type=_F32).astype(_BF16)


def _inproj(x, mod3, norm_g, w_in):
    bsz, s, d = x.shape
    return pl.pallas_call(
        _inproj_kernel,
        grid=(bsz, s // ROW_BLOCK),
        in_specs=[
            pl.BlockSpec((None, ROW_BLOCK, d), lambda b, i: (b, i, 0)),
            pl.BlockSpec((None, 1, 3 * d), lambda b, i: (b, 0, 0)),
            pl.BlockSpec((1, d), lambda b, i: (0, 0)),
            pl.BlockSpec((d, IN_WIDTH), lambda b, i: (0, 0)),
        ],
        out_specs=pl.BlockSpec((None, ROW_BLOCK, IN_WIDTH), lambda b, i: (b, i, 0)),
        out_shape=jax.ShapeDtypeStruct((bsz, s, IN_WIDTH), _BF16),
        scratch_shapes=[pltpu.VMEM((d, IN_WIDTH), _BF16)],
        compiler_params=pltpu.CompilerParams(
            dimension_semantics=("arbitrary", "arbitrary"), vmem_limit_bytes=VMEM_LIMIT),
        name="in_proj",
    )(x, mod3, norm_g.reshape(1, d), w_in)


def _sb_kernel(q_ref, k_ref, v_ref, g_ref, o_ref, u_ref, q2_ref, acc_ref, carry_ref):
    t = SB_TILE
    seq = q_ref.shape[0]
    row = lax.broadcasted_iota(jnp.int32, (t, t), 0)
    col = lax.broadcasted_iota(jnp.int32, (t, t), 1)
    u_ref[...] = jnp.where(row > col, 1.0, 0.0).astype(_BF16)
    row2 = lax.broadcasted_iota(jnp.int32, (2 * t, t), 0) & (t - 1)
    col2 = lax.broadcasted_iota(jnp.int32, (2 * t, t), 1)
    causal = col2 < row2
    head_a = lax.broadcasted_iota(jnp.int32, (t, LANES), 1) < HEAD_DIM

    def softplus_parts(z):
        zpos = jnp.maximum(z, 0.0)
        zneg = z - zpos
        lse = jnp.log(1.0 + jnp.exp(zneg - zpos))
        return zpos + lse, zneg - lse

    def scores(u, ks, width):
        k = k_ref[pl.ds(ks, width), :]
        return lax.dot_general(q2_ref[u], k, (((1,), (1,)), ((), ())), preferred_element_type=_F32)

    def later_sum(p):
        return jnp.dot(p.astype(_BF16), u_ref[...], preferred_element_type=_F32)

    def diagonal_only(u, qs):
        p, logb = softplus_parts(scores(u, qs, t))
        p = jnp.where(causal, p, 0.0)
        w = jnp.where(causal, jnp.exp(logb - later_sum(p)), 0.0)
        o = jnp.dot(w.astype(_BF16), v_ref[pl.ds(qs, t), :], preferred_element_type=_F32)
        return o, -jnp.sum(p, axis=1, keepdims=True)

    def window(u, qs):
        ws = qs - t if isinstance(qs, int) else pl.multiple_of(qs - t, t)
        p, logb = softplus_parts(scores(u, ws, 2 * t))
        p_prev, p_diag = p[:, 0:t], jnp.where(causal, p[:, t:2 * t], 0.0)
        rs_diag = jnp.sum(p_diag, axis=1, keepdims=True)
        w_diag = jnp.where(causal, jnp.exp(logb[:, t:2 * t] - later_sum(p_diag)), 0.0)
        w_prev = jnp.exp((logb[:, 0:t] - rs_diag) - later_sum(p_prev))
        w = jnp.concatenate([w_prev.astype(_BF16), w_diag.astype(_BF16)], axis=1)
        o = jnp.dot(w, v_ref[pl.ds(ws, 2 * t), :], preferred_element_type=_F32)
        return o, -(rs_diag + jnp.sum(p_prev, axis=1, keepdims=True))

    def tile(u, kj):
        ks = pl.multiple_of(kj * t, t)
        p, logb = softplus_parts(scores(u, ks, t))
        w = jnp.exp(logb - later_sum(p))
        o = jnp.dot(w.astype(_BF16), v_ref[pl.ds(ks, t), :], preferred_element_type=_F32)
        return o, jnp.sum(p, axis=1, keepdims=True)

    def load_q(u, qs):
        q = q_ref[pl.ds(qs, t), :] * jnp.asarray(HEAD_DIM ** -0.5, _BF16)
        zero = jnp.zeros_like(q)
        q2_ref[u] = jnp.concatenate([jnp.where(head_a, q, zero), jnp.where(head_a, zero, q)], axis=0)

    def finish(u, qs):
        acc = acc_ref[u]
        y = jnp.where(head_a, acc[0:t], acc[t:2 * t])
        g = g_ref[pl.ds(qs, t), :].astype(_F32)
        o_ref[pl.ds(qs, t), :] = (y * _silu(g)).astype(_BF16)

    def q_group(first_block, has_prev):
        starts, carry_max = [], []
        for u in range(SB_GROUP):
            qi = first_block + u
            qs = qi * t if isinstance(qi, int) else pl.multiple_of(qi * t, t)
            starts.append(qs)
            load_q(u, qs)
            o, carry = window(u, qs) if (has_prev or u > 0) else diagonal_only(u, qs)
            acc_ref[u] = o
            carry_ref[u] = carry
            carry_max.append(jnp.max(carry))
        for u in range(SB_GROUP):
            qi = first_block + u

            def more(state):
                kj, cmax = state
                return jnp.logical_and(kj >= 0, cmax > -SB_SKIP)

            def k_block(state, u=u):
                kj, _ = state
                o, rs = tile(u, kj)
                c = carry_ref[u]
                acc_ref[u] += jnp.exp(c) * o
                c = c - rs
                carry_ref[u] = c
                return kj - 1, jnp.max(c)

            lax.while_loop(more, k_block, (qi - 2, carry_max[u]))
        for u in range(SB_GROUP):
            finish(u, starts[u])

    q_group(0, False)

    def body(i, _):
        q_group(i * SB_GROUP, True)
        return 0

    lax.fori_loop(1, seq // (t * SB_GROUP), body, 0)


def _sb_attention(proj):
    bsz, s, _ = proj.shape
    t = SB_TILE

    def spec(blk):
        return pl.BlockSpec((None, s, LANES), lambda b, j, blk=blk: (b, 0, blk + j))

    return pl.pallas_call(
        _sb_kernel,
        grid=(bsz, PAIRS),
        in_specs=[spec(SB_Q_BLK), spec(SB_K_BLK), spec(SB_V_BLK), spec(SB_G_BLK)],
        out_specs=pl.BlockSpec((None, s, LANES), lambda b, j: (b, 0, j)),
        out_shape=jax.ShapeDtypeStruct((bsz, s, SB_WIDTH), _BF16),
        scratch_shapes=[
            pltpu.VMEM((t, t), _BF16),
            pltpu.VMEM((SB_GROUP, 2 * t, LANES), _BF16),
            pltpu.VMEM((SB_GROUP, 2 * t, LANES), _F32),
            pltpu.VMEM((SB_GROUP, 2 * t, 1), _F32),
        ],
        compiler_params=pltpu.CompilerParams(
            dimension_semantics=("parallel", "parallel"), vmem_limit_bytes=VMEM_LIMIT),
        name="sb_attention",
    )(proj, proj, proj, proj)


def _swout_kernel(sinks_ref, q_ref, g_lo_ref, g_hi_ref, k_ref, v_ref, x_ref, ysb_ref, mod_ref, w_ref, fg_ref,
                  o_ref, bias_ref, ksel_ref, vext_ref):
    w = WINDOW
    seq = k_ref.shape[0]
    i = pl.program_id(1)
    head_a = lax.broadcasted_iota(jnp.int32, (w, LANES), 1) < HEAD_DIM

    @pl.when(i == 0)
    def _():
        chunk = SW_PREP_ROWS
        low = lax.broadcasted_iota(jnp.int32, (chunk, LANES), 1) < HEAD_DIM
        for kvh in range(SW_KV_HEADS):
            ksel_ref[kvh, 0:w, :] = jnp.zeros((w, LANES), _BF16)
            vext_ref[kvh, 0:w, :] = jnp.zeros((w, 2 * LANES), _BF16)

        def prep(c, _):
            rs = pl.multiple_of(c * chunk, chunk)
            kc = k_ref[pl.ds(rs, chunk), :]
            vc = v_ref[pl.ds(rs, chunk), :]
            kr = pltpu.roll(kc, HEAD_DIM, 1)
            vr = pltpu.roll(vc, HEAD_DIM, 1)
            ones = jnp.ones((chunk, LANES), _BF16)
            dst = pl.ds(rs + w, chunk)
            ksel_ref[0, dst, :] = jnp.where(low, kc, kr)
            ksel_ref[1, dst, :] = jnp.where(low, kr, kc)
            vext_ref[0, dst, 0:LANES] = jnp.where(low, vc, vr)
            vext_ref[1, dst, 0:LANES] = jnp.where(low, vr, vc)
            vext_ref[0, dst, LANES:2 * LANES] = ones
            vext_ref[1, dst, LANES:2 * LANES] = ones
            return 0

        lax.fori_loop(0, seq // chunk, prep, 0)

        row = lax.broadcasted_iota(jnp.int32, (2 * w, 2 * w), 0)
        col = lax.broadcasted_iota(jnp.int32, (2 * w, 2 * w), 1)
        rel = w + (row & (w - 1)) - col
        valid = (rel >= 0) & (rel < w)
        for j in range(PAIRS):
            slope = jnp.where(row < w, _alibi_slope(2 * j), _alibi_slope(2 * j + 1))
            bias = jnp.where(valid, -slope * rel.astype(_F32), NEG_INF)
            bias_ref[j, 0] = bias
            bias_ref[j, 1] = jnp.where(col < w, NEG_INF, bias)

    rowc = lax.broadcasted_iota(jnp.int32, (2 * w, 1), 0)
    gate = mod_ref[:, 2 * D_MODEL:3 * D_MODEL]
    scale = jnp.asarray(HEAD_DIM ** -0.5, _BF16)

    def attend(r0, j, n):
        q = q_ref[r0:r0 + w, j * LANES:(j + 1) * LANES] * scale
        zero = jnp.zeros_like(q)
        q2 = jnp.concatenate([jnp.where(head_a, q, zero), jnp.where(head_a, zero, q)], axis=0)
        kvh = j // (PAIRS // SW_KV_HEADS)
        ks = pl.multiple_of(n * w, w)
        k = ksel_ref[kvh, pl.ds(ks, 2 * w), :]
        v = vext_ref[kvh, pl.ds(ks, 2 * w), :]
        bias = bias_ref[j, jnp.where(n == 0, 1, 0)]
        logits = lax.dot_general(q2, k, (((1,), (1,)), ((), ())), preferred_element_type=_F32) + bias
        mx = jnp.max(logits, axis=1, keepdims=True)
        p = jnp.exp(logits - mx)
        oe = jnp.dot(p.astype(_BF16), v, preferred_element_type=_F32)
        sink = jnp.where(rowc < w, sinks_ref[2 * j], sinks_ref[2 * j + 1])
        den2 = oe[:, LANES:2 * LANES] + jnp.exp(sink - mx)
        y = jnp.where(head_a, oe[0:w, 0:LANES], oe[w:2 * w, 0:LANES]) / jnp.where(head_a, den2[0:w], den2[w:2 * w])
        g_ref_j = g_lo_ref if j < PAIRS // 2 else g_hi_ref
        gl = (j % (PAIRS // 2)) * LANES
        g = g_ref_j[r0:r0 + w, gl:gl + LANES].astype(_F32)
        return (y * _silu(g)).astype(_BF16)

    for r0 in range(0, FUSE_ROWS, OUT_ROW_CHUNK):
        blocks = []
        for b0 in range(r0, r0 + OUT_ROW_CHUNK, w):
            n = i * (FUSE_ROWS // w) + b0 // w
            blocks.append(jnp.concatenate([attend(b0, j, n) for j in range(PAIRS)], axis=1))
        y_sw = jnp.concatenate(blocks, axis=0)
        rows = slice(r0, r0 + OUT_ROW_CHUNK)
        y = jnp.dot(ysb_ref[rows, :], w_ref[0:SB_WIDTH, :], preferred_element_type=_F32)
        y = y + jnp.dot(y_sw, w_ref[SB_WIDTH:SB_WIDTH + SW_WIDTH, :], preferred_element_type=_F32)
        xn = x_ref[rows, :] + gate * y
        r = lax.rsqrt(jnp.mean(xn * xn, axis=-1, keepdims=True) + RMS_EPS)
        o_ref[rows, :] = (xn * r) * fg_ref[...]


def _alibi_slope(head):
    return 2.0 ** (-8.0 * (head + 1) / SW_HEADS)


def _sw_outproj(proj, x, y_sb, mod3, sinks, w_out_bf16, final_g):
    bsz, s, d = x.shape
    rb = FUSE_ROWS
    half = SW_WIDTH // 2
    q_blk = (SW_Q_BLK * LANES) // SW_WIDTH
    g_blk = (SW_G_BLK * LANES) // half
    return pl.pallas_call(
        _swout_kernel,
        grid=(bsz, s // rb),
        in_specs=[
            pl.BlockSpec(memory_space=pltpu.SMEM),
            pl.BlockSpec((None, rb, SW_WIDTH), lambda b, i: (b, i, q_blk)),
            pl.BlockSpec((None, rb, half), lambda b, i: (b, i, g_blk)),
            pl.BlockSpec((None, rb, half), lambda b, i: (b, i, g_blk + 1)),
            pl.BlockSpec((None, s, LANES), lambda b, i: (b, 0, SW_K_BLK)),
            pl.BlockSpec((None, s, LANES), lambda b, i: (b, 0, SW_V_BLK)),
            pl.BlockSpec((None, rb, d), lambda b, i: (b, i, 0)),
            pl.BlockSpec((None, rb, SB_WIDTH), lambda b, i: (b, i, 0)),
            pl.BlockSpec((None, 1, 3 * d), lambda b, i: (b, 0, 0)),
            pl.BlockSpec((SB_WIDTH + SW_WIDTH, d), lambda b, i: (0, 0)),
            pl.BlockSpec((1, d), lambda b, i: (0, 0)),
        ],
        out_specs=pl.BlockSpec((None, rb, d), lambda b, i: (b, i, 0)),
        out_shape=jax.ShapeDtypeStruct((bsz, s, d), _F32),
        scratch_shapes=[
            pltpu.VMEM((PAIRS, 2, 2 * WINDOW, 2 * WINDOW), _F32),
            pltpu.VMEM((SW_KV_HEADS, s + WINDOW, LANES), _BF16),
            pltpu.VMEM((SW_KV_HEADS, s + WINDOW, 2 * LANES), _BF16),
        ],
        compiler_params=pltpu.CompilerParams(
            dimension_semantics=("parallel", "arbitrary"), vmem_limit_bytes=VMEM_LIMIT),
        name="sw_attention_out_proj",
    )(sinks, proj, proj, proj, proj, proj, x, y_sb, mod3, w_out_bf16, final_g.reshape(1, d))


def kernel(x, c, w_ada, b_ada, norm_g, w_in, sinks, w_out, final_g):
    assert w_ada.shape[0] == 1, "the output projection kernel fuses the final norm: single layer only"
    bsz = x.shape[0]
    mod3 = _adaln(c, w_ada[0], b_ada[0]).reshape(bsz, 1, 3 * D_MODEL)
    proj = _inproj(x, mod3, norm_g[0], w_in[0])
    y_sb = _sb_attention(proj)
    return _sw_outproj(proj, x, y_sb, mod3, sinks[0], w_out[0].astype(_BF16), final_g)
```

```python
import jax
import jax.numpy as jnp
from jax import lax
from jax.experimental import pallas as pl
from jax.experimental.pallas import tpu as pltpu

D_MODEL = 1024
HEAD_DIM = 64
SB_HEADS = 8
SW_HEADS = 8
SW_KV_HEADS = 2
SB_WIDTH = SB_HEADS * HEAD_DIM
SW_WIDTH = SW_HEADS * HEAD_DIM
SW_KV_WIDTH = SW_KV_HEADS * HEAD_DIM
IN_WIDTH = 4 * SB_WIDTH + 2 * SW_WIDTH + 2 * SW_KV_WIDTH
WINDOW = 128
RMS_EPS = 1e-6
NEG_INF = -1e30

LANES = 128
PAIRS = SB_WIDTH // LANES
SB_Q_BLK, SB_K_BLK, SB_V_BLK, SB_G_BLK = 0, PAIRS, 2 * PAIRS, 3 * PAIRS
SW_Q_BLK = 4 * PAIRS
SW_K_BLK = SW_Q_BLK + PAIRS
SW_V_BLK = SW_K_BLK + 1
SW_G_BLK = SW_V_BLK + 1

ROW_BLOCK = 1024
ROW_CHUNK = 512
FUSE_ROWS = 1024
OUT_ROW_CHUNK = 512
PROJ_COLS = 256
SB_TILE = 256
SB_GROUP = 16
SW_PREP_ROWS = 512
SB_SKIP = 105.0
VMEM_LIMIT = 48 * 1024 * 1024

_F32 = jnp.float32
_BF16 = jnp.bfloat16


def _silu(v):
    return v / (1.0 + jnp.exp(-v))


def _adaln_kernel(c_ref, w_ref, b_ref, o_ref):
    cond = _silu(c_ref[...])
    o_ref[...] = jnp.dot(cond, w_ref[...], preferred_element_type=_F32) + b_ref[...]


def _adaln(c, w_ada, b_ada):
    bsz, d = c.shape
    n = w_ada.shape[1]
    bn = 512
    return pl.pallas_call(
        _adaln_kernel,
        grid=(n // bn,),
        in_specs=[
            pl.BlockSpec((bsz, d), lambda j: (0, 0)),
            pl.BlockSpec((d, bn), lambda j: (0, j)),
            pl.BlockSpec((1, bn), lambda j: (0, j)),
        ],
        out_specs=pl.BlockSpec((bsz, bn), lambda j: (0, j)),
        out_shape=jax.ShapeDtypeStruct((bsz, n), _F32),
        name="adaln_mod",
    )(c, w_ada, b_ada.reshape(1, n))


def _inproj_kernel(x_ref, mod_ref, g_ref, w_ref, o_ref, wb_ref):
    @pl.when((pl.program_id(0) == 0) & (pl.program_id(1) == 0))
    def _():
        for n0 in range(0, IN_WIDTH, PROJ_COLS):
            wb_ref[:, n0:n0 + PROJ_COLS] = w_ref[:, n0:n0 + PROJ_COLS].astype(_BF16)

    shift = mod_ref[:, 0:D_MODEL]
    scale = mod_ref[:, D_MODEL:2 * D_MODEL]
    for r0 in range(0, ROW_BLOCK, ROW_CHUNK):
        x = x_ref[r0:r0 + ROW_CHUNK, :]
        r = lax.rsqrt(jnp.mean(x * x, axis=-1, keepdims=True) + RMS_EPS)
        h = ((x * r) * g_ref[...]) * (1.0 + scale) + shift
        hb = h.astype(_BF16)
        for n0 in range(0, IN_WIDTH, PROJ_COLS):
            o_ref[r0:r0 + ROW_CHUNK, n0:n0 + PROJ_COLS] = jnp.dot(
                hb, wb_ref[:, n0:n0 + PROJ_COLS], preferred_element_type=_F32).astype(_BF16)


def _inproj(x, mod3, norm_g, w_in):
    bsz, s, d = x.shape
    return pl.pallas_call(
        _inproj_kernel,
        grid=(bsz, s // ROW_BLOCK),
        in_specs=[
            pl.BlockSpec((None, ROW_BLOCK, d), lambda b, i: (b, i, 0)),
            pl.BlockSpec((None, 1, 3 * d), lambda b, i: (b, 0, 0)),
            pl.BlockSpec((1, d), lambda b, i: (0, 0)),
            pl.BlockSpec((d, IN_WIDTH), lambda b, i: (0, 0)),
        ],
        out_specs=pl.BlockSpec((None, ROW_BLOCK, IN_WIDTH), lambda b, i: (b, i, 0)),
        out_shape=jax.ShapeDtypeStruct((bsz, s, IN_WIDTH), _BF16),
        scratch_shapes=[pltpu.VMEM((d, IN_WIDTH), _BF16)],
        compiler_params=pltpu.CompilerParams(
            dimension_semantics=("arbitrary", "arbitrary"), vmem_limit_bytes=VMEM_LIMIT),
        name="in_proj",
    )(x, mod3, norm_g.reshape(1, d), w_in)


def _sb_kernel(q_ref, k_ref, v_ref, g_ref, o_ref, u_ref, q2_ref, acc_ref, carry_ref):
    t = SB_TILE
    seq = q_ref.shape[0]
    row = lax.broadcasted_iota(jnp.int32, (t, t), 0)
    col = lax.broadcasted_iota(jnp.int32, (t, t), 1)
    u_ref[...] = jnp.where(row > col, 1.0, 0.0).astype(_BF16)
    row2 = lax.broadcasted_iota(jnp.int32, (2 * t, t), 0) & (t - 1)
    col2 = lax.broadcasted_iota(jnp.int32, (2 * t, t), 1)
    causal = col2 < row2
    head_a = lax.broadcasted_iota(jnp.int32, (t, LANES), 1) < HEAD_DIM

    def softplus_parts(z):
        zpos = jnp.maximum(z, 0.0)
        zneg = z - zpos
        lse = jnp.log(1.0 + jnp.exp(zneg - zpos))
        return zpos + lse, zneg - lse

    def scores(u, ks, width):
        k = k_ref[pl.ds(ks, width), :]
        return lax.dot_general(q2_ref[u], k, (((1,), (1,)), ((), ())), preferred_element_type=_F32)

    def later_sum(p):
        return jnp.dot(p.astype(_BF16), u_ref[...], preferred_element_type=_F32)

    def diagonal_only(u, qs):
        p, logb = softplus_parts(scores(u, qs, t))
        p = jnp.where(causal, p, 0.0)
        w = jnp.where(causal, jnp.exp(logb - later_sum(p)), 0.0)
        o = jnp.dot(w.astype(_BF16), v_ref[pl.ds(qs, t), :], preferred_element_type=_F32)
        return o, -jnp.sum(p, axis=1, keepdims=True)

    def window(u, qs):
        ws = qs - t if isinstance(qs, int) else pl.multiple_of(qs - t, t)
        p, logb = softplus_parts(scores(u, ws, 2 * t))
        p_prev, p_diag = p[:, 0:t], jnp.where(causal, p[:, t:2 * t], 0.0)
        rs_diag = jnp.sum(p_diag, axis=1, keepdims=True)
        w_diag = jnp.where(causal, jnp.exp(logb[:, t:2 * t] - later_sum(p_diag)), 0.0)
        w_prev = jnp.exp((logb[:, 0:t] - rs_diag) - later_sum(p_prev))
        w = jnp.concatenate([w_prev.astype(_BF16), w_diag.astype(_BF16)], axis=1)
        o = jnp.dot(w, v_ref[pl.ds(ws, 2 * t), :], preferred_element_type=_F32)
        return o, -(rs_diag + jnp.sum(p_prev, axis=1, keepdims=True))

    def tile(u, kj):
        ks = pl.multiple_of(kj * t, t)
        p, logb = softplus_parts(scores(u, ks, t))
        w = jnp.exp(logb - later_sum(p))
        o = jnp.dot(w.astype(_BF16), v_ref[pl.ds(ks, t), :], preferred_element_type=_F32)
        return o, jnp.sum(p, axis=1, keepdims=True)

    def load_q(u, qs):
        q = q_ref[pl.ds(qs, t), :] * jnp.asarray(HEAD_DIM ** -0.5, _BF16)
        zero = jnp.zeros_like(q)
        q2_ref[u] = jnp.concatenate([jnp.where(head_a, q, zero), jnp.where(head_a, zero, q)], axis=0)

    def finish(u, qs):
        acc = acc_ref[u]
        y = jnp.where(head_a, acc[0:t], acc[t:2 * t])
        g = g_ref[pl.ds(qs, t), :].astype(_F32)
        o_ref[pl.ds(qs, t), :] = (y * _silu(g)).astype(_BF16)

    def q_group(first_block, has_prev):
        starts, carry_max = [], []
        for u in range(SB_GROUP):
            qi = first_block + u
            qs = qi * t if isinstance(qi, int) else pl.multiple_of(qi * t, t)
            starts.append(qs)
            load_q(u, qs)
            o, carry = window(u, qs) if (has_prev or u > 0) else diagonal_only(u, qs)
            acc_ref[u] = o
            carry_ref[u] = carry
            carry_max.append(jnp.max(carry))
        for u in range(SB_GROUP):
            qi = first_block + u

            def more(state):
                kj, cmax = state
                return jnp.logical_and(kj >= 0, cmax > -SB_SKIP)

            def k_block(state, u=u):
                kj, _ = state
                o, rs = tile(u, kj)
                c = carry_ref[u]
                acc_ref[u] += jnp.exp(c) * o
                c = c - rs
                carry_ref[u] = c
                return kj - 1, jnp.max(c)

            lax.while_loop(more, k_block, (qi - 2, carry_max[u]))
        for u in range(SB_GROUP):
            finish(u, starts[u])

    q_group(0, False)

    def body(i, _):
        q_group(i * SB_GROUP, True)
        return 0

    lax.fori_loop(1, seq // (t * SB_GROUP), body, 0)


def _sb_attention(proj):
    bsz, s, _ = proj.shape
    t = SB_TILE

    def spec(blk):
        return pl.BlockSpec((None, s, LANES), lambda b, j, blk=blk: (b, 0, blk + j))

    return pl.pallas_call(
        _sb_kernel,
        grid=(bsz, PAIRS),
        in_specs=[spec(SB_Q_BLK), spec(SB_K_BLK), spec(SB_V_BLK), spec(SB_G_BLK)],
        out_specs=pl.BlockSpec((None, s, LANES), lambda b, j: (b, 0, j)),
        out_shape=jax.ShapeDtypeStruct((bsz, s, SB_WIDTH), _BF16),
        scratch_shapes=[
            pltpu.VMEM((t, t), _BF16),
            pltpu.VMEM((SB_GROUP, 2 * t, LANES), _BF16),
            pltpu.VMEM((SB_GROUP, 2 * t, LANES), _F32),
            pltpu.VMEM((SB_GROUP, 2 * t, 1), _F32),
        ],
        compiler_params=pltpu.CompilerParams(
            dimension_semantics=("parallel", "parallel"), vmem_limit_bytes=VMEM_LIMIT),
        name="sb_attention",
    )(proj, proj, proj, proj)


def _swout_kernel(sinks_ref, q_ref, g_lo_ref, g_hi_ref, k_ref, v_ref, x_ref, ysb_ref, mod_ref, w_ref, fg_ref,
                  o_ref, bias_ref, ksel_ref, vext_ref):
    w = WINDOW
    seq = k_ref.shape[0]
    i = pl.program_id(1)
    head_a = lax.broadcasted_iota(jnp.int32, (w, LANES), 1) < HEAD_DIM

    @pl.when(i == 0)
    def _():
        chunk = SW_PREP_ROWS
        low = lax.broadcasted_iota(jnp.int32, (chunk, LANES), 1) < HEAD_DIM
        for kvh in range(SW_KV_HEADS):
            ksel_ref[kvh, 0:w, :] = jnp.zeros((w, LANES), _BF16)
            vext_ref[kvh, 0:w, :] = jnp.zeros((w, 2 * LANES), _BF16)

        def prep(c, _):
            rs = pl.multiple_of(c * chunk, chunk)
            kc = k_ref[pl.ds(rs, chunk), :]
            vc = v_ref[pl.ds(rs, chunk), :]
            kr = pltpu.roll(kc, HEAD_DIM, 1)
            vr = pltpu.roll(vc, HEAD_DIM, 1)
            ones = jnp.ones((chunk, LANES), _BF16)
            dst = pl.ds(rs + w, chunk)
            ksel_ref[0, dst, :] = jnp.where(low, kc, kr)
            ksel_ref[1, dst, :] = jnp.where(low, kr, kc)
            vext_ref[0, dst, 0:LANES] = jnp.where(low, vc, vr)
            vext_ref[1, dst, 0:LANES] = jnp.where(low, vr, vc)
            vext_ref[0, dst, LANES:2 * LANES] = ones
            vext_ref[1, dst, LANES:2 * LANES] = ones
            return 0

        lax.fori_loop(0, seq // chunk, prep, 0)

        row = lax.broadcasted_iota(jnp.int32, (2 * w, 2 * w), 0)
        col = lax.broadcasted_iota(jnp.int32, (2 * w, 2 * w), 1)
        rel = w + (row & (w - 1)) - col
        valid = (rel >= 0) & (rel < w)
        for j in range(PAIRS):
            slope = jnp.where(row < w, _alibi_slope(2 * j), _alibi_slope(2 * j + 1))
            bias = jnp.where(valid, -slope * rel.astype(_F32), NEG_INF)
            bias_ref[j, 0] = bias
            bias_ref[j, 1] = jnp.where(col < w, NEG_INF, bias)

    rowc = lax.broadcasted_iota(jnp.int32, (2 * w, 1), 0)
    gate = mod_ref[:, 2 * D_MODEL:3 * D_MODEL]
    scale = jnp.asarray(HEAD_DIM ** -0.5, _BF16)

    def attend(r0, j, n):
        q = q_ref[r0:r0 + w, j * LANES:(j + 1) * LANES] * scale
        zero = jnp.zeros_like(q)
        q2 = jnp.concatenate([jnp.where(head_a, q, zero), jnp.where(head_a, zero, q)], axis=0)
        kvh = j // (PAIRS // SW_KV_HEADS)
        ks = pl.multiple_of(n * w, w)
        k = ksel_ref[kvh, pl.ds(ks, 2 * w), :]
        v = vext_ref[kvh, pl.ds(ks, 2 * w), :]
        bias = bias_ref[j, jnp.where(n == 0, 1, 0)]
        logits = lax.dot_general(q2, k, (((1,), (1,)), ((), ())), preferred_element_type=_F32) + bias
        mx = jnp.max(logits, axis=1, keepdims=True)
        p = jnp.exp(logits - mx)
        oe = jnp.dot(p.astype(_BF16), v, preferred_element_type=_F32)
        sink = jnp.where(rowc < w, sinks_ref[2 * j], sinks_ref[2 * j + 1])
        den2 = oe[:, LANES:2 * LANES] + jnp.exp(sink - mx)
        y = jnp.where(head_a, oe[0:w, 0:LANES], oe[w:2 * w, 0:LANES]) / jnp.where(head_a, den2[0:w], den2[w:2 * w])
        g_ref_j = g_lo_ref if j < PAIRS // 2 else g_hi_ref
        gl = (j % (PAIRS // 2)) * LANES
        g = g_ref_j[r0:r0 + w, gl:gl + LANES].astype(_F32)
        return (y * _silu(g)).astype(_BF16)

    for r0 in range(0, FUSE_ROWS, OUT_ROW_CHUNK):
        blocks = []
        for b0 in range(r0, r0 + OUT_ROW_CHUNK, w):
            n = i * (FUSE_ROWS // w) + b0 // w
            blocks.append(jnp.concatenate([attend(b0, j, n) for j in range(PAIRS)], axis=1))
        y_sw = jnp.concatenate(blocks, axis=0)
        rows = slice(r0, r0 + OUT_ROW_CHUNK)
        y = jnp.dot(jnp.concatenate([ysb_ref[rows, :], y_sw], axis=1), w_ref[...], preferred_element_type=_F32)
        xn = x_ref[rows, :] + gate * y
        r = lax.rsqrt(jnp.mean(xn * xn, axis=-1, keepdims=True) + RMS_EPS)
        o_ref[rows, :] = (xn * r) * fg_ref[...]


def _alibi_slope(head):
    return 2.0 ** (-8.0 * (head + 1) / SW_HEADS)


def _sw_outproj(proj, x, y_sb, mod3, sinks, w_out_bf16, final_g):
    bsz, s, d = x.shape
    rb = FUSE_ROWS
    half = SW_WIDTH // 2
    q_blk = (SW_Q_BLK * LANES) // SW_WIDTH
    g_blk = (SW_G_BLK * LANES) // half
    return pl.pallas_call(
        _swout_kernel,
        grid=(bsz, s // rb),
        in_specs=[
            pl.BlockSpec(memory_space=pltpu.SMEM),
            pl.BlockSpec((None, rb, SW_WIDTH), lambda b, i: (b, i, q_blk)),
            pl.BlockSpec((None, rb, half), lambda b, i: (b, i, g_blk)),
            pl.BlockSpec((None, rb, half), lambda b, i: (b, i, g_blk + 1)),
            pl.BlockSpec((None, s, LANES), lambda b, i: (b, 0, SW_K_BLK)),
            pl.BlockSpec((None, s, LANES), lambda b, i: (b, 0, SW_V_BLK)),
            pl.BlockSpec((None, rb, d), lambda b, i: (b, i, 0)),
            pl.BlockSpec((None, rb, SB_WIDTH), lambda b, i: (b, i, 0)),
            pl.BlockSpec((None, 1, 3 * d), lambda b, i: (b, 0, 0)),
            pl.BlockSpec((SB_WIDTH + SW_WIDTH, d), lambda b, i: (0, 0)),
            pl.BlockSpec((1, d), lambda b, i: (0, 0)),
        ],
        out_specs=pl.BlockSpec((None, rb, d), lambda b, i: (b, i, 0)),
        out_shape=jax.ShapeDtypeStruct((bsz, s, d), _F32),
        scratch_shapes=[
            pltpu.VMEM((PAIRS, 2, 2 * WINDOW, 2 * WINDOW), _F32),
            pltpu.VMEM((SW_KV_HEADS, s + WINDOW, LANES), _BF16),
            pltpu.VMEM((SW_KV_HEADS, s + WINDOW, 2 * LANES), _BF16),
        ],
        compiler_params=pltpu.CompilerParams(
            dimension_semantics=("parallel", "arbitrary"), vmem_limit_bytes=VMEM_LIMIT),
        name="sw_attention_out_proj",
    )(sinks, proj, proj, proj, proj, proj, x, y_sb, mod3, w_out_bf16, final_g.reshape(1, d))


def kernel(x, c, w_ada, b_ada, norm_g, w_in, sinks, w_out, final_g):
    assert w_ada.shape[0] == 1, "the output projection kernel fuses the final norm: single layer only"
    bsz = x.shape[0]
    mod3 = _adaln(c, w_ada[0], b_ada[0]).reshape(bsz, 1, 3 * D_MODEL)
    proj = _inproj(x, mod3, norm_g[0], w_in[0])
    y_sb = _sb_attention(proj)
    return _sw_outproj(proj, x, y_sb, mod3, sinks[0], w_out[0].astype(_BF16), final_g)
```

```python
import jax
import jax.numpy as jnp
from jax import lax
from jax.experimental import pallas as pl
from jax.experimental.pallas import tpu as pltpu

D_MODEL = 1024
HEAD_DIM = 64
SB_HEADS = 8
SW_HEADS = 8
SW_KV_HEADS = 2
SB_WIDTH = SB_HEADS * HEAD_DIM
SW_WIDTH = SW_HEADS * HEAD_DIM
SW_KV_WIDTH = SW_KV_HEADS * HEAD_DIM
IN_WIDTH = 4 * SB_WIDTH + 2 * SW_WIDTH + 2 * SW_KV_WIDTH
WINDOW = 128
RMS_EPS = 1e-6
NEG_INF = -1e30

LANES = 128
PAIRS = SB_WIDTH // LANES
SB_Q_BLK, SB_K_BLK, SB_V_BLK, SB_G_BLK = 0, PAIRS, 2 * PAIRS, 3 * PAIRS
SW_Q_BLK = 4 * PAIRS
SW_K_BLK = SW_Q_BLK + PAIRS
SW_V_BLK = SW_K_BLK + 1
SW_G_BLK = SW_V_BLK + 1

ROW_BLOCK = 1024
ROW_CHUNK = 512
FUSE_ROWS = 1024
OUT_ROW_CHUNK = 512
PROJ_COLS = 256
SB_TILE = 256
SB_GROUP = 16
SW_PREP_ROWS = 512
SB_SKIP = 105.0
VMEM_LIMIT = 48 * 1024 * 1024

_F32 = jnp.float32
_BF16 = jnp.bfloat16


def _silu(v):
    return v / (1.0 + jnp.exp(-v))


def _is_gate_column(col):
    sb_gate = 3 * SB_WIDTH <= col < 4 * SB_WIDTH
    sw_gate = IN_WIDTH - SW_WIDTH <= col < IN_WIDTH
    return sb_gate or sw_gate


def _adaln_kernel(c_ref, w_ref, b_ref, o_ref):
    cond = _silu(c_ref[...])
    o_ref[...] = jnp.dot(cond, w_ref[...], preferred_element_type=_F32) + b_ref[...]


def _adaln(c, w_ada, b_ada):
    bsz, d = c.shape
    n = w_ada.shape[1]
    bn = 512
    return pl.pallas_call(
        _adaln_kernel,
        grid=(n // bn,),
        in_specs=[
            pl.BlockSpec((bsz, d), lambda j: (0, 0)),
            pl.BlockSpec((d, bn), lambda j: (0, j)),
            pl.BlockSpec((1, bn), lambda j: (0, j)),
        ],
        out_specs=pl.BlockSpec((bsz, bn), lambda j: (0, j)),
        out_shape=jax.ShapeDtypeStruct((bsz, n), _F32),
        name="adaln_mod",
    )(c, w_ada, b_ada.reshape(1, n))


def _inproj_kernel(x_ref, mod_ref, g_ref, w_ref, o_ref, wb_ref):
    @pl.when((pl.program_id(0) == 0) & (pl.program_id(1) == 0))
    def _():
        for n0 in range(0, IN_WIDTH, PROJ_COLS):
            wb_ref[:, n0:n0 + PROJ_COLS] = w_ref[:, n0:n0 + PROJ_COLS].astype(_BF16)

    shift = mod_ref[:, 0:D_MODEL]
    scale = mod_ref[:, D_MODEL:2 * D_MODEL]
    for r0 in range(0, ROW_BLOCK, ROW_CHUNK):
        x = x_ref[r0:r0 + ROW_CHUNK, :]
        r = lax.rsqrt(jnp.mean(x * x, axis=-1, keepdims=True) + RMS_EPS)
        h = ((x * r) * g_ref[...]) * (1.0 + scale) + shift
        hb = h.astype(_BF16)
        for n0 in range(0, IN_WIDTH, PROJ_COLS):
            acc = jnp.dot(hb, wb_ref[:, n0:n0 + PROJ_COLS], preferred_element_type=_F32)
            if _is_gate_column(n0):
                acc = _silu(acc)
            o_ref[r0:r0 + ROW_CHUNK, n0:n0 + PROJ_COLS] = acc.astype(_BF16)


def _inproj(x, mod3, norm_g, w_in):
    bsz, s, d = x.shape
    return pl.pallas_call(
        _inproj_kernel,
        grid=(bsz, s // ROW_BLOCK),
        in_specs=[
            pl.BlockSpec((None, ROW_BLOCK, d), lambda b, i: (b, i, 0)),
            pl.BlockSpec((None, 1, 3 * d), lambda b, i: (b, 0, 0)),
            pl.BlockSpec((1, d), lambda b, i: (0, 0)),
            pl.BlockSpec((d, IN_WIDTH), lambda b, i: (0, 0)),
        ],
        out_specs=pl.BlockSpec((None, ROW_BLOCK, IN_WIDTH), lambda b, i: (b, i, 0)),
        out_shape=jax.ShapeDtypeStruct((bsz, s, IN_WIDTH), _BF16),
        scratch_shapes=[pltpu.VMEM((d, IN_WIDTH), _BF16)],
        compiler_params=pltpu.CompilerParams(
            dimension_semantics=("arbitrary", "arbitrary"), vmem_limit_bytes=VMEM_LIMIT),
        name="in_proj",
    )(x, mod3, norm_g.reshape(1, d), w_in)


def _sb_kernel(q_ref, k_ref, v_ref, g_ref, o_ref, u_ref, q2_ref, acc_ref, carry_ref):
    t = SB_TILE
    seq = q_ref.shape[0]
    row = lax.broadcasted_iota(jnp.int32, (t, t), 0)
    col = lax.broadcasted_iota(jnp.int32, (t, t), 1)
    u_ref[...] = jnp.where(row > col, 1.0, 0.0).astype(_BF16)
    row2 = lax.broadcasted_iota(jnp.int32, (2 * t, t), 0) & (t - 1)
    col2 = lax.broadcasted_iota(jnp.int32, (2 * t, t), 1)
    causal = col2 < row2
    head_a = lax.broadcasted_iota(jnp.int32, (t, LANES), 1) < HEAD_DIM

    def softplus_parts(z):
        zpos = jnp.maximum(z, 0.0)
        zneg = z - zpos
        lse = jnp.log(1.0 + jnp.exp(zneg - zpos))
        return zpos + lse, zneg - lse

    def scores(u, ks, width):
        k = k_ref[pl.ds(ks, width), :]
        return lax.dot_general(q2_ref[u], k, (((1,), (1,)), ((), ())), preferred_element_type=_F32)

    def later_sum(p):
        return jnp.dot(p.astype(_BF16), u_ref[...], preferred_element_type=_F32)

    def diagonal_only(u, qs):
        p, logb = softplus_parts(scores(u, qs, t))
        p = jnp.where(causal, p, 0.0)
        w = jnp.where(causal, jnp.exp(logb - later_sum(p)), 0.0)
        o = jnp.dot(w.astype(_BF16), v_ref[pl.ds(qs, t), :], preferred_element_type=_F32)
        return o, -jnp.sum(p, axis=1, keepdims=True)

    def window(u, qs):
        ws = qs - t if isinstance(qs, int) else pl.multiple_of(qs - t, t)
        p, logb = softplus_parts(scores(u, ws, 2 * t))
        p_prev, p_diag = p[:, 0:t], jnp.where(causal, p[:, t:2 * t], 0.0)
        rs_diag = jnp.sum(p_diag, axis=1, keepdims=True)
        w_diag = jnp.where(causal, jnp.exp(logb[:, t:2 * t] - later_sum(p_diag)), 0.0)
        w_prev = jnp.exp((logb[:, 0:t] - rs_diag) - later_sum(p_prev))
        w = jnp.concatenate([w_prev.astype(_BF16), w_diag.astype(_BF16)], axis=1)
        o = jnp.dot(w, v_ref[pl.ds(ws, 2 * t), :], preferred_element_type=_F32)
        return o, -(rs_diag + jnp.sum(p_prev, axis=1, keepdims=True))

    def tile(u, kj):
        ks = pl.multiple_of(kj * t, t)
        p, logb = softplus_parts(scores(u, ks, t))
        w = jnp.exp(logb - later_sum(p))
        o = jnp.dot(w.astype(_BF16), v_ref[pl.ds(ks, t), :], preferred_element_type=_F32)
        return o, jnp.sum(p, axis=1, keepdims=True)

    def load_q(u, qs):
        q = q_ref[pl.ds(qs, t), :] * jnp.asarray(HEAD_DIM ** -0.5, _BF16)
        zero = jnp.zeros_like(q)
        q2_ref[u] = jnp.concatenate([jnp.where(head_a, q, zero), jnp.where(head_a, zero, q)], axis=0)

    def finish(u, qs):
        acc = acc_ref[u]
        y = jnp.where(head_a, acc[0:t], acc[t:2 * t])
        g = g_ref[pl.ds(qs, t), :].astype(_F32)
        o_ref[pl.ds(qs, t), :] = (y * g).astype(_BF16)

    def q_group(first_block, has_prev):
        starts, carry_max = [], []
        for u in range(SB_GROUP):
            qi = first_block + u
            qs = qi * t if isinstance(qi, int) else pl.multiple_of(qi * t, t)
            starts.append(qs)
            load_q(u, qs)
            o, carry = window(u, qs) if (has_prev or u > 0) else diagonal_only(u, qs)
            acc_ref[u] = o
            carry_ref[u] = carry
            carry_max.append(jnp.max(carry))
        for u in range(SB_GROUP):
            qi = first_block + u

            def more(state):
                kj, cmax = state
                return jnp.logical_and(kj >= 0, cmax > -SB_SKIP)

            def k_block(state, u=u):
                kj, _ = state
                o, rs = tile(u, kj)
                c = carry_ref[u]
                acc_ref[u] += jnp.exp(c) * o
                c = c - rs
                carry_ref[u] = c
                return kj - 1, jnp.max(c)

            lax.while_loop(more, k_block, (qi - 2, carry_max[u]))
        for u in range(SB_GROUP):
            finish(u, starts[u])

    q_group(0, False)

    def body(i, _):
        q_group(i * SB_GROUP, True)
        return 0

    lax.fori_loop(1, seq // (t * SB_GROUP), body, 0)


def _sb_attention(proj):
    bsz, s, _ = proj.shape
    t = SB_TILE

    def spec(blk):
        return pl.BlockSpec((None, s, LANES), lambda b, j, blk=blk: (b, 0, blk + j))

    return pl.pallas_call(
        _sb_kernel,
        grid=(bsz, PAIRS),
        in_specs=[spec(SB_Q_BLK), spec(SB_K_BLK), spec(SB_V_BLK), spec(SB_G_BLK)],
        out_specs=pl.BlockSpec((None, s, LANES), lambda b, j: (b, 0, j)),
        out_shape=jax.ShapeDtypeStruct((bsz, s, SB_WIDTH), _BF16),
        scratch_shapes=[
            pltpu.VMEM((t, t), _BF16),
            pltpu.VMEM((SB_GROUP, 2 * t, LANES), _BF16),
            pltpu.VMEM((SB_GROUP, 2 * t, LANES), _F32),
            pltpu.VMEM((SB_GROUP, 2 * t, 1), _F32),
        ],
        compiler_params=pltpu.CompilerParams(
            dimension_semantics=("parallel", "parallel"), vmem_limit_bytes=VMEM_LIMIT),
        name="sb_attention",
    )(proj, proj, proj, proj)


def _swout_kernel(sinks_ref, q_ref, g_lo_ref, g_hi_ref, k_ref, v_ref, x_ref, ysb_ref, mod_ref, w_ref, fg_ref,
                  o_ref, bias_ref, ksel_ref, vext_ref):
    w = WINDOW
    seq = k_ref.shape[0]
    i = pl.program_id(1)
    head_a = lax.broadcasted_iota(jnp.int32, (w, LANES), 1) < HEAD_DIM

    @pl.when(i == 0)
    def _():
        chunk = SW_PREP_ROWS
        low = lax.broadcasted_iota(jnp.int32, (chunk, LANES), 1) < HEAD_DIM
        for kvh in range(SW_KV_HEADS):
            ksel_ref[kvh, 0:w, :] = jnp.zeros((w, LANES), _BF16)
            vext_ref[kvh, 0:w, :] = jnp.zeros((w, 2 * LANES), _BF16)

        def prep(c, _):
            rs = pl.multiple_of(c * chunk, chunk)
            kc = k_ref[pl.ds(rs, chunk), :]
            vc = v_ref[pl.ds(rs, chunk), :]
            kr = pltpu.roll(kc, HEAD_DIM, 1)
            vr = pltpu.roll(vc, HEAD_DIM, 1)
            ones = jnp.ones((chunk, LANES), _BF16)
            dst = pl.ds(rs + w, chunk)
            ksel_ref[0, dst, :] = jnp.where(low, kc, kr)
            ksel_ref[1, dst, :] = jnp.where(low, kr, kc)
            vext_ref[0, dst, 0:LANES] = jnp.where(low, vc, vr)
            vext_ref[1, dst, 0:LANES] = jnp.where(low, vr, vc)
            vext_ref[0, dst, LANES:2 * LANES] = ones
            vext_ref[1, dst, LANES:2 * LANES] = ones
            return 0

        lax.fori_loop(0, seq // chunk, prep, 0)

        row = lax.broadcasted_iota(jnp.int32, (2 * w, 2 * w), 0)
        col = lax.broadcasted_iota(jnp.int32, (2 * w, 2 * w), 1)
        rel = w + (row & (w - 1)) - col
        valid = (rel >= 0) & (rel < w)
        for j in range(PAIRS):
            slope = jnp.where(row < w, _alibi_slope(2 * j), _alibi_slope(2 * j + 1))
            bias = jnp.where(valid, -slope * rel.astype(_F32), NEG_INF)
            bias_ref[j, 0] = bias
            bias_ref[j, 1] = jnp.where(col < w, NEG_INF, bias)

    rowc = lax.broadcasted_iota(jnp.int32, (2 * w, 1), 0)
    gate = mod_ref[:, 2 * D_MODEL:3 * D_MODEL]
    scale = jnp.asarray(HEAD_DIM ** -0.5, _BF16)

    def attend(r0, j, n):
        q = q_ref[r0:r0 + w, j * LANES:(j + 1) * LANES] * scale
        zero = jnp.zeros_like(q)
        q2 = jnp.concatenate([jnp.where(head_a, q, zero), jnp.where(head_a, zero, q)], axis=0)
        kvh = j // (PAIRS // SW_KV_HEADS)
        ks = pl.multiple_of(n * w, w)
        k = ksel_ref[kvh, pl.ds(ks, 2 * w), :]
        v = vext_ref[kvh, pl.ds(ks, 2 * w), :]
        bias = bias_ref[j, jnp.where(n == 0, 1, 0)]
        logits = lax.dot_general(q2, k, (((1,), (1,)), ((), ())), preferred_element_type=_F32) + bias
        mx = jnp.max(logits, axis=1, keepdims=True)
        p = jnp.exp(logits - mx)
        oe = jnp.dot(p.astype(_BF16), v, preferred_element_type=_F32)
        sink = jnp.where(rowc < w, sinks_ref[2 * j], sinks_ref[2 * j + 1])
        den2 = oe[:, LANES:2 * LANES] + jnp.exp(sink - mx)
        y = jnp.where(head_a, oe[0:w, 0:LANES], oe[w:2 * w, 0:LANES]) / jnp.where(head_a, den2[0:w], den2[w:2 * w])
        g_ref_j = g_lo_ref if j < PAIRS // 2 else g_hi_ref
        gl = (j % (PAIRS // 2)) * LANES
        g = g_ref_j[r0:r0 + w, gl:gl + LANES].astype(_F32)
        return (y * g).astype(_BF16)

    for r0 in range(0, FUSE_ROWS, OUT_ROW_CHUNK):
        blocks = []
        for b0 in range(r0, r0 + OUT_ROW_CHUNK, w):
            n = i * (FUSE_ROWS // w) + b0 // w
            blocks.append(jnp.concatenate([attend(b0, j, n) for j in range(PAIRS)], axis=1))
        y_sw = jnp.concatenate(blocks, axis=0)
        rows = slice(r0, r0 + OUT_ROW_CHUNK)
        y = jnp.dot(jnp.concatenate([ysb_ref[rows, :], y_sw], axis=1), w_ref[...], preferred_element_type=_F32)
        xn = x_ref[rows, :] + gate * y
        r = lax.rsqrt(jnp.mean(xn * xn, axis=-1, keepdims=True) + RMS_EPS)
        o_ref[rows, :] = (xn * r) * fg_ref[...]


def _alibi_slope(head):
    return 2.0 ** (-8.0 * (head + 1) / SW_HEADS)


def _sw_outproj(proj, x, y_sb, mod3, sinks, w_out_bf16, final_g):
    bsz, s, d = x.shape
    rb = FUSE_ROWS
    half = SW_WIDTH // 2
    q_blk = (SW_Q_BLK * LANES) // SW_WIDTH
    g_blk = (SW_G_BLK * LANES) // half
    return pl.pallas_call(
        _swout_kernel,
        grid=(bsz, s // rb),
        in_specs=[
            pl.BlockSpec(memory_space=pltpu.SMEM),
            pl.BlockSpec((None, rb, SW_WIDTH), lambda b, i: (b, i, q_blk)),
            pl.BlockSpec((None, rb, half), lambda b, i: (b, i, g_blk)),
            pl.BlockSpec((None, rb, half), lambda b, i: (b, i, g_blk + 1)),
            pl.BlockSpec((None, s, LANES), lambda b, i: (b, 0, SW_K_BLK)),
            pl.BlockSpec((None, s, LANES), lambda b, i: (b, 0, SW_V_BLK)),
            pl.BlockSpec((None, rb, d), lambda b, i: (b, i, 0)),
            pl.BlockSpec((None, rb, SB_WIDTH), lambda b, i: (b, i, 0)),
            pl.BlockSpec((None, 1, 3 * d), lambda b, i: (b, 0, 0)),
            pl.BlockSpec((SB_WIDTH + SW_WIDTH, d), lambda b, i: (0, 0)),
            pl.BlockSpec((1, d), lambda b, i: (0, 0)),
        ],
        out_specs=pl.BlockSpec((None, rb, d), lambda b, i: (b, i, 0)),
        out_shape=jax.ShapeDtypeStruct((bsz, s, d), _F32),
        scratch_shapes=[
            pltpu.VMEM((PAIRS, 2, 2 * WINDOW, 2 * WINDOW), _F32),
            pltpu.VMEM((SW_KV_HEADS, s + WINDOW, LANES), _BF16),
            pltpu.VMEM((SW_KV_HEADS, s + WINDOW, 2 * LANES), _BF16),
        ],
        compiler_params=pltpu.CompilerParams(
            dimension_semantics=("parallel", "arbitrary"), vmem_limit_bytes=VMEM_LIMIT),
        name="sw_attention_out_proj",
    )(sinks, proj, proj, proj, proj, proj, x, y_sb, mod3, w_out_bf16, final_g.reshape(1, d))


def kernel(x, c, w_ada, b_ada, norm_g, w_in, sinks, w_out, final_g):
    assert w_ada.shape[0] == 1, "the output projection kernel fuses the final norm: single layer only"
    bsz = x.shape[0]
    mod3 = _adaln(c, w_ada[0], b_ada[0]).reshape(bsz, 1, 3 * D_MODEL)
    proj = _inproj(x, mod3, norm_g[0], w_in[0])
    y_sb = _sb_attention(proj)
    return _sw_outproj(proj, x, y_sb, mod3, sinks[0], w_out[0].astype(_BF16), final_g)
```

```python
import jax
import jax.numpy as jnp
from jax import lax
from jax.experimental import pallas as pl
from jax.experimental.pallas import tpu as pltpu

D_MODEL = 1024
HEAD_DIM = 64
SB_HEADS = 8
SW_HEADS = 8
SW_KV_HEADS = 2
SB_WIDTH = SB_HEADS * HEAD_DIM
SW_WIDTH = SW_HEADS * HEAD_DIM
SW_KV_WIDTH = SW_KV_HEADS * HEAD_DIM
IN_WIDTH = 4 * SB_WIDTH + 2 * SW_WIDTH + 2 * SW_KV_WIDTH
WINDOW = 128
RMS_EPS = 1e-6
NEG_INF = -1e30

LANES = 128
PAIRS = SB_WIDTH // LANES
SB_Q_BLK, SB_K_BLK, SB_V_BLK, SB_G_BLK = 0, PAIRS, 2 * PAIRS, 3 * PAIRS
SW_Q_BLK = 4 * PAIRS
SW_K_BLK = SW_Q_BLK + PAIRS
SW_V_BLK = SW_K_BLK + 1
SW_G_BLK = SW_V_BLK + 1

ROW_BLOCK = 1024
ROW_CHUNK = 256
FUSE_ROWS = 1024
OUT_ROW_CHUNK = 512
PROJ_COLS = 256
SB_TILE = 256
SB_GROUP = 16
SW_PREP_ROWS = 512
SB_SKIP = 105.0
VMEM_LIMIT = 48 * 1024 * 1024

_F32 = jnp.float32
_BF16 = jnp.bfloat16


def _silu(v):
    return v / (1.0 + jnp.exp(-v))


def _is_gate_column(col):
    sb_gate = 3 * SB_WIDTH <= col < 4 * SB_WIDTH
    sw_gate = IN_WIDTH - SW_WIDTH <= col < IN_WIDTH
    return sb_gate or sw_gate


def _adaln_kernel(c_ref, w_ref, b_ref, o_ref):
    cond = _silu(c_ref[...])
    o_ref[...] = jnp.dot(cond, w_ref[...], preferred_element_type=_F32) + b_ref[...]


def _adaln(c, w_ada, b_ada):
    bsz, d = c.shape
    n = w_ada.shape[1]
    bn = 512
    return pl.pallas_call(
        _adaln_kernel,
        grid=(n // bn,),
        in_specs=[
            pl.BlockSpec((bsz, d), lambda j: (0, 0)),
            pl.BlockSpec((d, bn), lambda j: (0, j)),
            pl.BlockSpec((1, bn), lambda j: (0, j)),
        ],
        out_specs=pl.BlockSpec((bsz, bn), lambda j: (0, j)),
        out_shape=jax.ShapeDtypeStruct((bsz, n), _F32),
        name="adaln_mod",
    )(c, w_ada, b_ada.reshape(1, n))


def _inproj_kernel(x_ref, mod_ref, g_ref, w_ref, o_ref, wb_ref):
    @pl.when((pl.program_id(0) == 0) & (pl.program_id(1) == 0))
    def _():
        for n0 in range(0, IN_WIDTH, PROJ_COLS):
            wb_ref[:, n0:n0 + PROJ_COLS] = w_ref[:, n0:n0 + PROJ_COLS].astype(_BF16)

    shift = mod_ref[:, 0:D_MODEL]
    scale = mod_ref[:, D_MODEL:2 * D_MODEL]
    for r0 in range(0, ROW_BLOCK, ROW_CHUNK):
        x = x_ref[r0:r0 + ROW_CHUNK, :]
        r = lax.rsqrt(jnp.mean(x * x, axis=-1, keepdims=True) + RMS_EPS)
        h = ((x * r) * g_ref[...]) * (1.0 + scale) + shift
        hb = h.astype(_BF16)
        for n0 in range(0, IN_WIDTH, PROJ_COLS):
            acc = jnp.dot(hb, wb_ref[:, n0:n0 + PROJ_COLS], preferred_element_type=_F32)
            if _is_gate_column(n0):
                acc = _silu(acc)
            o_ref[r0:r0 + ROW_CHUNK, n0:n0 + PROJ_COLS] = acc.astype(_BF16)


def _inproj(x, mod3, norm_g, w_in):
    bsz, s, d = x.shape
    return pl.pallas_call(
        _inproj_kernel,
        grid=(bsz, s // ROW_BLOCK),
        in_specs=[
            pl.BlockSpec((None, ROW_BLOCK, d), lambda b, i: (b, i, 0)),
            pl.BlockSpec((None, 1, 3 * d), lambda b, i: (b, 0, 0)),
            pl.BlockSpec((1, d), lambda b, i: (0, 0)),
            pl.BlockSpec((d, IN_WIDTH), lambda b, i: (0, 0)),
        ],
        out_specs=pl.BlockSpec((None, ROW_BLOCK, IN_WIDTH), lambda b, i: (b, i, 0)),
        out_shape=jax.ShapeDtypeStruct((bsz, s, IN_WIDTH), _BF16),
        scratch_shapes=[pltpu.VMEM((d, IN_WIDTH), _BF16)],
        compiler_params=pltpu.CompilerParams(
            dimension_semantics=("arbitrary", "arbitrary"), vmem_limit_bytes=VMEM_LIMIT),
        name="in_proj",
    )(x, mod3, norm_g.reshape(1, d), w_in)


def _sb_kernel(q_ref, k_ref, v_ref, g_ref, o_ref, u_ref, q2_ref, acc_ref, carry_ref):
    t = SB_TILE
    seq = q_ref.shape[0]
    row = lax.broadcasted_iota(jnp.int32, (t, t), 0)
    col = lax.broadcasted_iota(jnp.int32, (t, t), 1)
    u_ref[...] = jnp.where(row > col, 1.0, 0.0).astype(_BF16)
    row2 = lax.broadcasted_iota(jnp.int32, (2 * t, t), 0) & (t - 1)
    col2 = lax.broadcasted_iota(jnp.int32, (2 * t, t), 1)
    causal = col2 < row2
    head_a = lax.broadcasted_iota(jnp.int32, (t, LANES), 1) < HEAD_DIM

    def softplus_parts(z):
        zpos = jnp.maximum(z, 0.0)
        zneg = z - zpos
        lse = jnp.log(1.0 + jnp.exp(zneg - zpos))
        return zpos + lse, zneg - lse

    def scores(u, ks, width):
        k = k_ref[pl.ds(ks, width), :]
        return lax.dot_general(q2_ref[u], k, (((1,), (1,)), ((), ())), preferred_element_type=_F32)

    def later_sum(p):
        return jnp.dot(p.astype(_BF16), u_ref[...], preferred_element_type=_F32)

    def diagonal_only(u, qs):
        p, logb = softplus_parts(scores(u, qs, t))
        p = jnp.where(causal, p, 0.0)
        w = jnp.where(causal, jnp.exp(logb - later_sum(p)), 0.0)
        o = jnp.dot(w.astype(_BF16), v_ref[pl.ds(qs, t), :], preferred_element_type=_F32)
        return o, -jnp.sum(p, axis=1, keepdims=True)

    def window(u, qs):
        ws = qs - t if isinstance(qs, int) else pl.multiple_of(qs - t, t)
        p, logb = softplus_parts(scores(u, ws, 2 * t))
        p_prev, p_diag = p[:, 0:t], jnp.where(causal, p[:, t:2 * t], 0.0)
        rs_diag = jnp.sum(p_diag, axis=1, keepdims=True)
        w_diag = jnp.where(causal, jnp.exp(logb[:, t:2 * t] - later_sum(p_diag)), 0.0)
        w_prev = jnp.exp((logb[:, 0:t] - rs_diag) - later_sum(p_prev))
        w = jnp.concatenate([w_prev.astype(_BF16), w_diag.astype(_BF16)], axis=1)
        o = jnp.dot(w, v_ref[pl.ds(ws, 2 * t), :], preferred_element_type=_F32)
        return o, -(rs_diag + jnp.sum(p_prev, axis=1, keepdims=True))

    def tile(u, kj):
        ks = pl.multiple_of(kj * t, t)
        p, logb = softplus_parts(scores(u, ks, t))
        w = jnp.exp(logb - later_sum(p))
        o = jnp.dot(w.astype(_BF16), v_ref[pl.ds(ks, t), :], preferred_element_type=_F32)
        return o, jnp.sum(p, axis=1, keepdims=True)

    def load_q(u, qs):
        q = q_ref[pl.ds(qs, t), :] * jnp.asarray(HEAD_DIM ** -0.5, _BF16)
        zero = jnp.zeros_like(q)
        q2_ref[u] = jnp.concatenate([jnp.where(head_a, q, zero), jnp.where(head_a, zero, q)], axis=0)

    def finish(u, qs):
        acc = acc_ref[u]
        y = jnp.where(head_a, acc[0:t], acc[t:2 * t])
        g = g_ref[pl.ds(qs, t), :].astype(_F32)
        o_ref[pl.ds(qs, t), :] = (y * g).astype(_BF16)

    def q_group(first_block, has_prev):
        starts, carry_max = [], []
        for u in range(SB_GROUP):
            qi = first_block + u
            qs = qi * t if isinstance(qi, int) else pl.multiple_of(qi * t, t)
            starts.append(qs)
            load_q(u, qs)
            o, carry = window(u, qs) if (has_prev or u > 0) else diagonal_only(u, qs)
            acc_ref[u] = o
            carry_ref[u] = carry
            carry_max.append(jnp.max(carry))
        group_max = carry_max[0]
        for cm in carry_max[1:]:
            group_max = jnp.maximum(group_max, cm)

        @pl.when(group_max > -SB_SKIP)
        def _():
            for u in range(SB_GROUP):
                qi = first_block + u

                def more(state):
                    kj, cmax = state
                    return jnp.logical_and(kj >= 0, cmax > -SB_SKIP)

                def k_block(state, u=u):
                    kj, _ = state
                    o, rs = tile(u, kj)
                    c = carry_ref[u]
                    acc_ref[u] += jnp.exp(c) * o
                    c = c - rs
                    carry_ref[u] = c
                    return kj - 1, jnp.max(c)

                lax.while_loop(more, k_block, (qi - 2, carry_max[u]))

        for u in range(SB_GROUP):
            finish(u, starts[u])

    q_group(0, False)

    def body(i, _):
        q_group(i * SB_GROUP, True)
        return 0

    lax.fori_loop(1, seq // (t * SB_GROUP), body, 0)


def _sb_attention(proj):
    bsz, s, _ = proj.shape
    t = SB_TILE

    def spec(blk):
        return pl.BlockSpec((None, s, LANES), lambda b, j, blk=blk: (b, 0, blk + j))

    return pl.pallas_call(
        _sb_kernel,
        grid=(bsz, PAIRS),
        in_specs=[spec(SB_Q_BLK), spec(SB_K_BLK), spec(SB_V_BLK), spec(SB_G_BLK)],
        out_specs=pl.BlockSpec((None, s, LANES), lambda b, j: (b, 0, j)),
        out_shape=jax.ShapeDtypeStruct((bsz, s, SB_WIDTH), _BF16),
        scratch_shapes=[
            pltpu.VMEM((t, t), _BF16),
            pltpu.VMEM((SB_GROUP, 2 * t, LANES), _BF16),
            pltpu.VMEM((SB_GROUP, 2 * t, LANES), _F32),
            pltpu.VMEM((SB_GROUP, 2 * t, 1), _F32),
        ],
        compiler_params=pltpu.CompilerParams(
            dimension_semantics=("parallel", "parallel"), vmem_limit_bytes=VMEM_LIMIT),
        name="sb_attention",
    )(proj, proj, proj, proj)


def _swout_kernel(sinks_ref, q_ref, g_lo_ref, g_hi_ref, k_ref, v_ref, x_ref, ysb_ref, mod_ref, w_ref, fg_ref,
                  o_ref, bias_ref, ksel_ref, vext_ref):
    w = WINDOW
    seq = k_ref.shape[0]
    i = pl.program_id(1)
    head_a = lax.broadcasted_iota(jnp.int32, (w, LANES), 1) < HEAD_DIM

    @pl.when(i == 0)
    def _():
        chunk = SW_PREP_ROWS
        low = lax.broadcasted_iota(jnp.int32, (chunk, LANES), 1) < HEAD_DIM
        for kvh in range(SW_KV_HEADS):
            ksel_ref[kvh, 0:w, :] = jnp.zeros((w, LANES), _BF16)
            vext_ref[kvh, 0:w, :] = jnp.zeros((w, 2 * LANES), _BF16)

        def prep(c, _):
            rs = pl.multiple_of(c * chunk, chunk)
            kc = k_ref[pl.ds(rs, chunk), :]
            vc = v_ref[pl.ds(rs, chunk), :]
            kr = pltpu.roll(kc, HEAD_DIM, 1)
            vr = pltpu.roll(vc, HEAD_DIM, 1)
            ones = jnp.ones((chunk, LANES), _BF16)
            dst = pl.ds(rs + w, chunk)
            ksel_ref[0, dst, :] = jnp.where(low, kc, kr)
            ksel_ref[1, dst, :] = jnp.where(low, kr, kc)
            vext_ref[0, dst, 0:LANES] = jnp.where(low, vc, vr)
            vext_ref[1, dst, 0:LANES] = jnp.where(low, vr, vc)
            vext_ref[0, dst, LANES:2 * LANES] = ones
            vext_ref[1, dst, LANES:2 * LANES] = ones
            return 0

        lax.fori_loop(0, seq // chunk, prep, 0)

        row = lax.broadcasted_iota(jnp.int32, (2 * w, 2 * w), 0)
        col = lax.broadcasted_iota(jnp.int32, (2 * w, 2 * w), 1)
        rel = w + (row & (w - 1)) - col
        valid = (rel >= 0) & (rel < w)
        for j in range(PAIRS):
            slope = jnp.where(row < w, _alibi_slope(2 * j), _alibi_slope(2 * j + 1))
            bias = jnp.where(valid, -slope * rel.astype(_F32), NEG_INF)
            bias_ref[j, 0] = bias
            bias_ref[j, 1] = jnp.where(col < w, NEG_INF, bias)

    rowc = lax.broadcasted_iota(jnp.int32, (2 * w, 1), 0)
    gate = mod_ref[:, 2 * D_MODEL:3 * D_MODEL]
    scale = jnp.asarray(HEAD_DIM ** -0.5, _BF16)

    def attend(r0, j, n):
        q = q_ref[r0:r0 + w, j * LANES:(j + 1) * LANES] * scale
        zero = jnp.zeros_like(q)
        q2 = jnp.concatenate([jnp.where(head_a, q, zero), jnp.where(head_a, zero, q)], axis=0)
        kvh = j // (PAIRS // SW_KV_HEADS)
        ks = pl.multiple_of(n * w, w)
        k = ksel_ref[kvh, pl.ds(ks, 2 * w), :]
        v = vext_ref[kvh, pl.ds(ks, 2 * w), :]
        bias = bias_ref[j, jnp.where(n == 0, 1, 0)]
        logits = lax.dot_general(q2, k, (((1,), (1,)), ((), ())), preferred_element_type=_F32) + bias
        mx = jnp.max(logits, axis=1, keepdims=True)
        p = jnp.exp(logits - mx)
        oe = jnp.dot(p.astype(_BF16), v, preferred_element_type=_F32)
        sink = jnp.where(rowc < w, sinks_ref[2 * j], sinks_ref[2 * j + 1])
        den2 = oe[:, LANES:2 * LANES] + jnp.exp(sink - mx)
        y = jnp.where(head_a, oe[0:w, 0:LANES], oe[w:2 * w, 0:LANES]) / jnp.where(head_a, den2[0:w], den2[w:2 * w])
        g_ref_j = g_lo_ref if j < PAIRS // 2 else g_hi_ref
        gl = (j % (PAIRS // 2)) * LANES
        g = g_ref_j[r0:r0 + w, gl:gl + LANES].astype(_F32)
        return (y * g).astype(_BF16)

    for r0 in range(0, FUSE_ROWS, OUT_ROW_CHUNK):
        blocks = []
        for b0 in range(r0, r0 + OUT_ROW_CHUNK, w):
            n = i * (FUSE_ROWS // w) + b0 // w
            blocks.append(jnp.concatenate([attend(b0, j, n) for j in range(PAIRS)], axis=1))
        y_sw = jnp.concatenate(blocks, axis=0)
        rows = slice(r0, r0 + OUT_ROW_CHUNK)
        y = jnp.dot(jnp.concatenate([ysb_ref[rows, :], y_sw], axis=1), w_ref[...], preferred_element_type=_F32)
        xn = x_ref[rows, :] + gate * y
        r = lax.rsqrt(jnp.mean(xn * xn, axis=-1, keepdims=True) + RMS_EPS)
        o_ref[rows, :] = (xn * r) * fg_ref[...]


def _alibi_slope(head):
    return 2.0 ** (-8.0 * (head + 1) / SW_HEADS)


def _sw_outproj(proj, x, y_sb, mod3, sinks, w_out_bf16, final_g):
    bsz, s, d = x.shape
    rb = FUSE_ROWS
    half = SW_WIDTH // 2
    q_blk = (SW_Q_BLK * LANES) // SW_WIDTH
    g_blk = (SW_G_BLK * LANES) // half
    return pl.pallas_call(
        _swout_kernel,
        grid=(bsz, s // rb),
        in_specs=[
            pl.BlockSpec(memory_space=pltpu.SMEM),
            pl.BlockSpec((None, rb, SW_WIDTH), lambda b, i: (b, i, q_blk)),
            pl.BlockSpec((None, rb, half), lambda b, i: (b, i, g_blk)),
            pl.BlockSpec((None, rb, half), lambda b, i: (b, i, g_blk + 1)),
            pl.BlockSpec((None, s, LANES), lambda b, i: (b, 0, SW_K_BLK)),
            pl.BlockSpec((None, s, LANES), lambda b, i: (b, 0, SW_V_BLK)),
            pl.BlockSpec((None, rb, d), lambda b, i: (b, i, 0)),
            pl.BlockSpec((None, rb, SB_WIDTH), lambda b, i: (b, i, 0)),
            pl.BlockSpec((None, 1, 3 * d), lambda b, i: (b, 0, 0)),
            pl.BlockSpec((SB_WIDTH + SW_WIDTH, d), lambda b, i: (0, 0)),
            pl.BlockSpec((1, d), lambda b, i: (0, 0)),
        ],
        out_specs=pl.BlockSpec((None, rb, d), lambda b, i: (b, i, 0)),
        out_shape=jax.ShapeDtypeStruct((bsz, s, d), _F32),
        scratch_shapes=[
            pltpu.VMEM((PAIRS, 2, 2 * WINDOW, 2 * WINDOW), _F32),
            pltpu.VMEM((SW_KV_HEADS, s + WINDOW, LANES), _BF16),
            pltpu.VMEM((SW_KV_HEADS, s + WINDOW, 2 * LANES), _BF16),
        ],
        compiler_params=pltpu.CompilerParams(
            dimension_semantics=("parallel", "arbitrary"), vmem_limit_bytes=VMEM_LIMIT),
        name="sw_attention_out_proj",
    )(sinks, proj, proj, proj, proj, proj, x, y_sb, mod3, w_out_bf16, final_g.reshape(1, d))


def kernel(x, c, w_ada, b_ada, norm_g, w_in, sinks, w_out, final_g):
    assert w_ada.shape[0] == 1, "the output projection kernel fuses the final norm: single layer only"
    bsz = x.shape[0]
    mod3 = _adaln(c, w_ada[0], b_ada[0]).reshape(bsz, 1, 3 * D_MODEL)
    proj = _inproj(x, mod3, norm_g[0], w_in[0])
    y_sb = _sb_attention(proj)
    return _sw_outproj(proj, x, y_sb, mod3, sinks[0], w_out[0].astype(_BF16), final_g)
```

```python
import jax
import jax.numpy as jnp
from jax import lax
from jax.experimental import pallas as pl
from jax.experimental.pallas import tpu as pltpu

D_MODEL = 1024
HEAD_DIM = 64
SB_HEADS = 8
SW_HEADS = 8
SW_KV_HEADS = 2
SB_WIDTH = SB_HEADS * HEAD_DIM
SW_WIDTH = SW_HEADS * HEAD_DIM
SW_KV_WIDTH = SW_KV_HEADS * HEAD_DIM
IN_WIDTH = 4 * SB_WIDTH + 2 * SW_WIDTH + 2 * SW_KV_WIDTH
WINDOW = 128
RMS_EPS = 1e-6
NEG_INF = -1e30

LANES = 128
PAIRS = SB_WIDTH // LANES
SB_Q_BLK, SB_K_BLK, SB_V_BLK, SB_G_BLK = 0, PAIRS, 2 * PAIRS, 3 * PAIRS
SW_Q_BLK = 4 * PAIRS
SW_K_BLK = SW_Q_BLK + PAIRS
SW_V_BLK = SW_K_BLK + 1
SW_G_BLK = SW_V_BLK + 1

ROW_BLOCK = 1024
ROW_CHUNK = 256
FUSE_ROWS = 1024
OUT_ROW_CHUNK = 512
PROJ_COLS = 256
SB_TILE = 256
SB_GROUP = 16
SW_PREP_ROWS = 512
SB_STEP_PAIRS = 2
SB_SKIP = 105.0
VMEM_LIMIT = 48 * 1024 * 1024

_F32 = jnp.float32
_BF16 = jnp.bfloat16


def _silu(v):
    return v / (1.0 + jnp.exp(-v))


def _is_gate_column(col):
    sb_gate = 3 * SB_WIDTH <= col < 4 * SB_WIDTH
    sw_gate = IN_WIDTH - SW_WIDTH <= col < IN_WIDTH
    return sb_gate or sw_gate


def _adaln_kernel(c_ref, w_ref, b_ref, o_ref):
    cond = _silu(c_ref[...])
    o_ref[...] = jnp.dot(cond, w_ref[...], preferred_element_type=_F32) + b_ref[...]


def _adaln(c, w_ada, b_ada):
    bsz, d = c.shape
    n = w_ada.shape[1]
    bn = 512
    return pl.pallas_call(
        _adaln_kernel,
        grid=(n // bn,),
        in_specs=[
            pl.BlockSpec((bsz, d), lambda j: (0, 0)),
            pl.BlockSpec((d, bn), lambda j: (0, j)),
            pl.BlockSpec((1, bn), lambda j: (0, j)),
        ],
        out_specs=pl.BlockSpec((bsz, bn), lambda j: (0, j)),
        out_shape=jax.ShapeDtypeStruct((bsz, n), _F32),
        name="adaln_mod",
    )(c, w_ada, b_ada.reshape(1, n))


def _inproj_kernel(x_ref, mod_ref, g_ref, w_ref, o_ref, wb_ref):
    @pl.when((pl.program_id(0) == 0) & (pl.program_id(1) == 0))
    def _():
        for n0 in range(0, IN_WIDTH, PROJ_COLS):
            wb_ref[:, n0:n0 + PROJ_COLS] = w_ref[:, n0:n0 + PROJ_COLS].astype(_BF16)

    shift = mod_ref[:, 0:D_MODEL]
    scale = mod_ref[:, D_MODEL:2 * D_MODEL]
    for r0 in range(0, ROW_BLOCK, ROW_CHUNK):
        x = x_ref[r0:r0 + ROW_CHUNK, :]
        r = lax.rsqrt(jnp.mean(x * x, axis=-1, keepdims=True) + RMS_EPS)
        h = ((x * r) * g_ref[...]) * (1.0 + scale) + shift
        hb = h.astype(_BF16)
        for n0 in range(0, IN_WIDTH, PROJ_COLS):
            acc = jnp.dot(hb, wb_ref[:, n0:n0 + PROJ_COLS], preferred_element_type=_F32)
            if _is_gate_column(n0):
                acc = _silu(acc)
            o_ref[r0:r0 + ROW_CHUNK, n0:n0 + PROJ_COLS] = acc.astype(_BF16)


def _inproj(x, mod3, norm_g, w_in):
    bsz, s, d = x.shape
    return pl.pallas_call(
        _inproj_kernel,
        grid=(bsz, s // ROW_BLOCK),
        in_specs=[
            pl.BlockSpec((None, ROW_BLOCK, d), lambda b, i: (b, i, 0)),
            pl.BlockSpec((None, 1, 3 * d), lambda b, i: (b, 0, 0)),
            pl.BlockSpec((1, d), lambda b, i: (0, 0)),
            pl.BlockSpec((d, IN_WIDTH), lambda b, i: (0, 0)),
        ],
        out_specs=pl.BlockSpec((None, ROW_BLOCK, IN_WIDTH), lambda b, i: (b, i, 0)),
        out_shape=jax.ShapeDtypeStruct((bsz, s, IN_WIDTH), _BF16),
        scratch_shapes=[pltpu.VMEM((d, IN_WIDTH), _BF16)],
        compiler_params=pltpu.CompilerParams(
            dimension_semantics=("arbitrary", "arbitrary"), vmem_limit_bytes=VMEM_LIMIT),
        name="in_proj",
    )(x, mod3, norm_g.reshape(1, d), w_in)


def _sb_kernel(q_ref, k_ref, v_ref, g_ref, o_ref, u_ref, q2_ref, acc_ref, carry_ref):
    t = SB_TILE
    seq = q_ref.shape[0]
    lanes = [None]
    row = lax.broadcasted_iota(jnp.int32, (t, t), 0)
    col = lax.broadcasted_iota(jnp.int32, (t, t), 1)
    u_ref[...] = jnp.where(row > col, 1.0, 0.0).astype(_BF16)
    row2 = lax.broadcasted_iota(jnp.int32, (2 * t, t), 0) & (t - 1)
    col2 = lax.broadcasted_iota(jnp.int32, (2 * t, t), 1)
    causal = col2 < row2
    head_a = lax.broadcasted_iota(jnp.int32, (t, LANES), 1) < HEAD_DIM

    def softplus_parts(z):
        zpos = jnp.maximum(z, 0.0)
        zneg = z - zpos
        lse = jnp.log(1.0 + jnp.exp(zneg - zpos))
        return zpos + lse, zneg - lse

    def scores(u, ks, width):
        k = k_ref[pl.ds(ks, width), lanes[0]]
        return lax.dot_general(q2_ref[u], k, (((1,), (1,)), ((), ())), preferred_element_type=_F32)

    def later_sum(p):
        return jnp.dot(p.astype(_BF16), u_ref[...], preferred_element_type=_F32)

    def diagonal_only(u, qs):
        p, logb = softplus_parts(scores(u, qs, t))
        p = jnp.where(causal, p, 0.0)
        w = jnp.where(causal, jnp.exp(logb - later_sum(p)), 0.0)
        o = jnp.dot(w.astype(_BF16), v_ref[pl.ds(qs, t), lanes[0]], preferred_element_type=_F32)
        return o, -jnp.sum(p, axis=1, keepdims=True)

    def window(u, qs):
        ws = qs - t if isinstance(qs, int) else pl.multiple_of(qs - t, t)
        p, logb = softplus_parts(scores(u, ws, 2 * t))
        p_prev, p_diag = p[:, 0:t], jnp.where(causal, p[:, t:2 * t], 0.0)
        rs_diag = jnp.sum(p_diag, axis=1, keepdims=True)
        w_diag = jnp.where(causal, jnp.exp(logb[:, t:2 * t] - later_sum(p_diag)), 0.0)
        w_prev = jnp.exp((logb[:, 0:t] - rs_diag) - later_sum(p_prev))
        w = jnp.concatenate([w_prev.astype(_BF16), w_diag.astype(_BF16)], axis=1)
        o = jnp.dot(w, v_ref[pl.ds(ws, 2 * t), lanes[0]], preferred_element_type=_F32)
        return o, -(rs_diag + jnp.sum(p_prev, axis=1, keepdims=True))

    def tile(u, kj):
        ks = pl.multiple_of(kj * t, t)
        p, logb = softplus_parts(scores(u, ks, t))
        w = jnp.exp(logb - later_sum(p))
        o = jnp.dot(w.astype(_BF16), v_ref[pl.ds(ks, t), lanes[0]], preferred_element_type=_F32)
        return o, jnp.sum(p, axis=1, keepdims=True)

    def load_q(u, qs):
        q = q_ref[pl.ds(qs, t), lanes[0]] * jnp.asarray(HEAD_DIM ** -0.5, _BF16)
        zero = jnp.zeros_like(q)
        q2_ref[u] = jnp.concatenate([jnp.where(head_a, q, zero), jnp.where(head_a, zero, q)], axis=0)

    def finish(u, qs):
        acc = acc_ref[u]
        y = jnp.where(head_a, acc[0:t], acc[t:2 * t])
        g = g_ref[pl.ds(qs, t), lanes[0]].astype(_F32)
        o_ref[pl.ds(qs, t), lanes[0]] = (y * g).astype(_BF16)

    def q_group(first_block, has_prev):
        starts, carry_max = [], []
        for u in range(SB_GROUP):
            qi = first_block + u
            qs = qi * t if isinstance(qi, int) else pl.multiple_of(qi * t, t)
            starts.append(qs)
            load_q(u, qs)
            o, carry = window(u, qs) if (has_prev or u > 0) else diagonal_only(u, qs)
            acc_ref[u] = o
            carry_ref[u] = carry
            carry_max.append(jnp.max(carry))
        group_max = carry_max[0]
        for cm in carry_max[1:]:
            group_max = jnp.maximum(group_max, cm)

        @pl.when(group_max > -SB_SKIP)
        def _():
            for u in range(SB_GROUP):
                qi = first_block + u

                def more(state):
                    kj, cmax = state
                    return jnp.logical_and(kj >= 0, cmax > -SB_SKIP)

                def k_block(state, u=u):
                    kj, _ = state
                    o, rs = tile(u, kj)
                    c = carry_ref[u]
                    acc_ref[u] += jnp.exp(c) * o
                    c = c - rs
                    carry_ref[u] = c
                    return kj - 1, jnp.max(c)

                lax.while_loop(more, k_block, (qi - 2, carry_max[u]))

        for u in range(SB_GROUP):
            finish(u, starts[u])

    def pair(j, _):
        lanes[0] = pl.ds(pl.multiple_of(j * LANES, LANES), LANES)
        q_group(0, False)

        def body(i, _):
            q_group(i * SB_GROUP, True)
            return 0

        lax.fori_loop(1, seq // (t * SB_GROUP), body, 0)
        return 0

    lax.fori_loop(0, SB_STEP_PAIRS, pair, 0)


def _sb_attention(proj):
    bsz, s, _ = proj.shape
    t = SB_TILE

    def spec(blk):
        first = blk // SB_STEP_PAIRS
        return pl.BlockSpec((None, s, SB_STEP_PAIRS * LANES), lambda b, jj, first=first: (b, 0, first + jj))

    return pl.pallas_call(
        _sb_kernel,
        grid=(bsz, PAIRS // SB_STEP_PAIRS),
        in_specs=[spec(SB_Q_BLK), spec(SB_K_BLK), spec(SB_V_BLK), spec(SB_G_BLK)],
        out_specs=pl.BlockSpec((None, s, SB_STEP_PAIRS * LANES), lambda b, jj: (b, 0, jj)),
        out_shape=jax.ShapeDtypeStruct((bsz, s, SB_WIDTH), _BF16),
        scratch_shapes=[
            pltpu.VMEM((t, t), _BF16),
            pltpu.VMEM((SB_GROUP, 2 * t, LANES), _BF16),
            pltpu.VMEM((SB_GROUP, 2 * t, LANES), _F32),
            pltpu.VMEM((SB_GROUP, 2 * t, 1), _F32),
        ],
        compiler_params=pltpu.CompilerParams(
            dimension_semantics=("parallel", "parallel"), vmem_limit_bytes=VMEM_LIMIT),
        name="sb_attention",
    )(proj, proj, proj, proj)


def _swout_kernel(sinks_ref, q_ref, g_lo_ref, g_hi_ref, k_ref, v_ref, x_ref, ysb_ref, mod_ref, w_ref, fg_ref,
                  o_ref, bias_ref, ksel_ref, vext_ref):
    w = WINDOW
    seq = k_ref.shape[0]
    i = pl.program_id(1)
    head_a = lax.broadcasted_iota(jnp.int32, (w, LANES), 1) < HEAD_DIM

    @pl.when(i == 0)
    def _():
        chunk = SW_PREP_ROWS
        low = lax.broadcasted_iota(jnp.int32, (chunk, LANES), 1) < HEAD_DIM
        for kvh in range(SW_KV_HEADS):
            ksel_ref[kvh, 0:w, :] = jnp.zeros((w, LANES), _BF16)
            vext_ref[kvh, 0:w, :] = jnp.zeros((w, 2 * LANES), _BF16)

        def prep(c, _):
            rs = pl.multiple_of(c * chunk, chunk)
            kc = k_ref[pl.ds(rs, chunk), :]
            vc = v_ref[pl.ds(rs, chunk), :]
            kr = pltpu.roll(kc, HEAD_DIM, 1)
            vr = pltpu.roll(vc, HEAD_DIM, 1)
            ones = jnp.ones((chunk, LANES), _BF16)
            dst = pl.ds(rs + w, chunk)
            ksel_ref[0, dst, :] = jnp.where(low, kc, kr)
            ksel_ref[1, dst, :] = jnp.where(low, kr, kc)
            vext_ref[0, dst, 0:LANES] = jnp.where(low, vc, vr)
            vext_ref[1, dst, 0:LANES] = jnp.where(low, vr, vc)
            vext_ref[0, dst, LANES:2 * LANES] = ones
            vext_ref[1, dst, LANES:2 * LANES] = ones
            return 0

        lax.fori_loop(0, seq // chunk, prep, 0)

        row = lax.broadcasted_iota(jnp.int32, (2 * w, 2 * w), 0)
        col = lax.broadcasted_iota(jnp.int32, (2 * w, 2 * w), 1)
        rel = w + (row & (w - 1)) - col
        valid = (rel >= 0) & (rel < w)
        for j in range(PAIRS):
            slope = jnp.where(row < w, _alibi_slope(2 * j), _alibi_slope(2 * j + 1))
            bias = jnp.where(valid, -slope * rel.astype(_F32), NEG_INF)
            bias_ref[j, 0] = bias
            bias_ref[j, 1] = jnp.where(col < w, NEG_INF, bias)

    rowc = lax.broadcasted_iota(jnp.int32, (2 * w, 1), 0)
    gate = mod_ref[:, 2 * D_MODEL:3 * D_MODEL]
    scale = jnp.asarray(HEAD_DIM ** -0.5, _BF16)

    def attend(r0, j, n):
        q = q_ref[r0:r0 + w, j * LANES:(j + 1) * LANES] * scale
        zero = jnp.zeros_like(q)
        q2 = jnp.concatenate([jnp.where(head_a, q, zero), jnp.where(head_a, zero, q)], axis=0)
        kvh = j // (PAIRS // SW_KV_HEADS)
        ks = pl.multiple_of(n * w, w)
        k = ksel_ref[kvh, pl.ds(ks, 2 * w), :]
        v = vext_ref[kvh, pl.ds(ks, 2 * w), :]
        bias = bias_ref[j, jnp.where(n == 0, 1, 0)]
        logits = lax.dot_general(q2, k, (((1,), (1,)), ((), ())), preferred_element_type=_F32) + bias
        mx = jnp.max(logits, axis=1, keepdims=True)
        p = jnp.exp(logits - mx)
        oe = jnp.dot(p.astype(_BF16), v, preferred_element_type=_F32)
        sink = jnp.where(rowc < w, sinks_ref[2 * j], sinks_ref[2 * j + 1])
        den2 = oe[:, LANES:2 * LANES] + jnp.exp(sink - mx)
        y = jnp.where(head_a, oe[0:w, 0:LANES], oe[w:2 * w, 0:LANES]) / jnp.where(head_a, den2[0:w], den2[w:2 * w])
        g_ref_j = g_lo_ref if j < PAIRS // 2 else g_hi_ref
        gl = (j % (PAIRS // 2)) * LANES
        g = g_ref_j[r0:r0 + w, gl:gl + LANES].astype(_F32)
        return (y * g).astype(_BF16)

    for r0 in range(0, FUSE_ROWS, OUT_ROW_CHUNK):
        blocks = []
        for b0 in range(r0, r0 + OUT_ROW_CHUNK, w):
            n = i * (FUSE_ROWS // w) + b0 // w
            blocks.append(jnp.concatenate([attend(b0, j, n) for j in range(PAIRS)], axis=1))
        y_sw = jnp.concatenate(blocks, axis=0)
        rows = slice(r0, r0 + OUT_ROW_CHUNK)
        y = jnp.dot(jnp.concatenate([ysb_ref[rows, :], y_sw], axis=1), w_ref[...], preferred_element_type=_F32)
        xn = x_ref[rows, :] + gate * y
        r = lax.rsqrt(jnp.mean(xn * xn, axis=-1, keepdims=True) + RMS_EPS)
        o_ref[rows, :] = (xn * r) * fg_ref[...]


def _alibi_slope(head):
    return 2.0 ** (-8.0 * (head + 1) / SW_HEADS)


def _sw_outproj(proj, x, y_sb, mod3, sinks, w_out_bf16, final_g):
    bsz, s, d = x.shape
    rb = FUSE_ROWS
    half = SW_WIDTH // 2
    q_blk = (SW_Q_BLK * LANES) // SW_WIDTH
    g_blk = (SW_G_BLK * LANES) // half
    return pl.pallas_call(
        _swout_kernel,
        grid=(bsz, s // rb),
        in_specs=[
            pl.BlockSpec(memory_space=pltpu.SMEM),
            pl.BlockSpec((None, rb, SW_WIDTH), lambda b, i: (b, i, q_blk)),
            pl.BlockSpec((None, rb, half), lambda b, i: (b, i, g_blk)),
            pl.BlockSpec((None, rb, half), lambda b, i: (b, i, g_blk + 1)),
            pl.BlockSpec((None, s, LANES), lambda b, i: (b, 0, SW_K_BLK)),
            pl.BlockSpec((None, s, LANES), lambda b, i: (b, 0, SW_V_BLK)),
            pl.BlockSpec((None, rb, d), lambda b, i: (b, i, 0)),
            pl.BlockSpec((None, rb, SB_WIDTH), lambda b, i: (b, i, 0)),
            pl.BlockSpec((None, 1, 3 * d), lambda b, i: (b, 0, 0)),
            pl.BlockSpec((SB_WIDTH + SW_WIDTH, d), lambda b, i: (0, 0)),
            pl.BlockSpec((1, d), lambda b, i: (0, 0)),
        ],
        out_specs=pl.BlockSpec((None, rb, d), lambda b, i: (b, i, 0)),
        out_shape=jax.ShapeDtypeStruct((bsz, s, d), _F32),
        scratch_shapes=[
            pltpu.VMEM((PAIRS, 2, 2 * WINDOW, 2 * WINDOW), _F32),
            pltpu.VMEM((SW_KV_HEADS, s + WINDOW, LANES), _BF16),
            pltpu.VMEM((SW_KV_HEADS, s + WINDOW, 2 * LANES), _BF16),
        ],
        compiler_params=pltpu.CompilerParams(
            dimension_semantics=("parallel", "arbitrary"), vmem_limit_bytes=VMEM_LIMIT),
        name="sw_attention_out_proj",
    )(sinks, proj, proj, proj, proj, proj, x, y_sb, mod3, w_out_bf16, final_g.reshape(1, d))


def kernel(x, c, w_ada, b_ada, norm_g, w_in, sinks, w_out, final_g):
    assert w_ada.shape[0] == 1, "the output projection kernel fuses the final norm: single layer only"
    bsz = x.shape[0]
    mod3 = _adaln(c, w_ada[0], b_ada[0]).reshape(bsz, 1, 3 * D_MODEL)
    proj = _inproj(x, mod3, norm_g[0], w_in[0])
    y_sb = _sb_attention(proj)
    return _sw_outproj(proj, x, y_sb, mod3, sinks[0], w_out[0].astype(_BF16), final_g)
```
